```python
import math
import jax
import jax.numpy as jnp
from jax import lax
import numpy as np

D_MODEL = 2048
BATCH = 4
SEQ = 4096
DEPTH = 4

N_BRANCH = 4
N_HEADS = 4
CHUNK = 64
EPS = 1e-6
GDN_DK = 128
GDN_DV = 128
CONV_W = 4
HGRN_DK = 128
HGRN_DV = 128
RET_DK = 128
RET_DV = 128
ROPE_BASE = 10000.0
GLA_DK = 64
GLA_DV = 128
GLA_LOWRANK = 16
GLA_GATE_NORM = 16.0
BRANCH_WIDTH = N_HEADS * 128
D_FF = -(-8 * D_MODEL // (3 * 256)) * 256
IN_SPLITS = (
    ("gdn_q", N_HEADS * GDN_DK), ("gdn_k", N_HEADS * GDN_DK), ("gdn_v", N_HEADS * GDN_DV),
    ("gdn_a", N_HEADS), ("gdn_b", N_HEADS), ("gdn_gate", N_HEADS * GDN_DV),
    ("hgrn_q", N_HEADS * HGRN_DK), ("hgrn_f", N_HEADS * HGRN_DK), ("hgrn_i", N_HEADS * HGRN_DV),
    ("hgrn_gate", N_HEADS * HGRN_DV),
    ("ret_q", N_HEADS * RET_DK), ("ret_k", N_HEADS * RET_DK), ("ret_v", N_HEADS * RET_DV),
    ("ret_gate", N_HEADS * RET_DV),
    ("gla_q", N_HEADS * GLA_DK), ("gla_k", N_HEADS * GLA_DK), ("gla_v", N_HEADS * GLA_DV),
    ("gla_gk_lr", GLA_LOWRANK), ("gla_gate", N_HEADS * GLA_DV),
)
D_IN = sum(w for _, w in IN_SPLITS)

kernel_name = "hybrid_gated_parallel_linear_mixers"

F32 = jnp.float32


def rms_norm(x, g):
    xf = x.astype(F32)
    y = xf * lax.rsqrt(jnp.mean(xf * xf, axis=-1, keepdims=True) + EPS)
    return (y * g.astype(F32)).astype(x.dtype)


def head_rms_norm(o, g):
    return o * lax.rsqrt(jnp.mean(o * o, axis=-1, keepdims=True) + EPS) * g.astype(F32)


def head_group_norm(o, g, b):
    mu = jnp.mean(o, axis=-1, keepdims=True)
    oc = o - mu
    var = jnp.mean(oc * oc, axis=-1, keepdims=True)
    return oc * lax.rsqrt(var + EPS) * g.astype(F32) + b.astype(F32)


def l2_normalize(x):
    return x * lax.rsqrt(jnp.sum(x * x, axis=-1, keepdims=True) + EPS)


def to_heads(z, d):
    b, t, _ = z.shape
    return z.reshape(b, t, -1, d).transpose(0, 2, 1, 3)


def gate_heads(o, gate):
    b, h, t, d = o.shape
    o = o.transpose(0, 2, 1, 3).reshape(b, t, h * d)
    return (o * jax.nn.silu(gate.astype(F32))).astype(gate.dtype)


def to_chunks(z):
    b, h, t, d = z.shape
    return z.reshape(b, h, t // CHUNK, CHUNK, d).transpose(2, 0, 1, 3, 4)


def from_chunks(z):
    n, b, h, c, d = z.shape
    return z.transpose(1, 2, 0, 3, 4).reshape(b, h, n * c, d)


def causal_masks():
    incl = jnp.tril(jnp.ones((CHUNK, CHUNK), dtype=bool))
    strict = jnp.tril(jnp.ones((CHUNK, CHUNK), dtype=bool), k=-1)
    return incl, strict


def split_columns(z):
    out = {}
    off = 0
    for name, w in IN_SPLITS:
        out[name] = z[..., off:off + w]
        off += w
    return out


def causal_depthwise_conv(x, w):
    width, ch = w.shape
    return lax.conv_general_dilated(
        x, w[:, None, :].astype(x.dtype), window_strides=(1,), padding=((width - 1, 0),),
        dimension_numbers=("NWC", "WIO", "NWC"), feature_group_count=ch)


def rotary(x, positions):
    half = x.shape[-1] // 2
    inv_freq = ROPE_BASE ** (-jnp.arange(half, dtype=F32) / half)
    ang = positions.astype(F32)[:, None, :, None] * inv_freq
    cos, sin = jnp.cos(ang), jnp.sin(ang)
    x1, x2 = x[..., :half], x[..., half:]
    return jnp.concatenate([x1 * cos - x2 * sin, x1 * sin + x2 * cos], axis=-1)


def gated_delta_rule_chunked(q, k, v, beta, g):
    incl, strict = causal_masks()
    q, k, v = to_chunks(q), to_chunks(k), to_chunks(v)
    beta = to_chunks(beta[..., None])
    gc = jnp.cumsum(to_chunks(g[..., None])[..., 0], axis=-1)
    decay = jnp.exp(jnp.where(incl, gc[..., :, None] - gc[..., None, :], -jnp.inf))
    kb = k * beta
    lower = jnp.where(strict, jnp.einsum("nbhid,nbhjd->nbhij", kb, k) * decay, 0.0)
    u = lax.linalg.triangular_solve(lower, v * beta, left_side=True, lower=True, unit_diagonal=True)
    w = lax.linalg.triangular_solve(lower, kb * jnp.exp(gc)[..., None], left_side=True, lower=True,
                                    unit_diagonal=True)
    a_qk = jnp.einsum("nbhid,nbhjd->nbhij", q, k) * decay
    q_dec = q * jnp.exp(gc)[..., None]
    k_dec = k * jnp.exp(gc[..., -1:] - gc)[..., None]
    g_last = jnp.exp(gc[..., -1])[..., None, None]

    def step(state, xs):
        u_n, w_n, a_n, qd_n, kd_n, gl_n = xs
        v_new = u_n - jnp.einsum("bhcd,bhde->bhce", w_n, state)
        o_n = jnp.einsum("bhcd,bhde->bhce", qd_n, state) + jnp.einsum("bhij,bhje->bhie", a_n, v_new)
        state = gl_n * state + jnp.einsum("bhcd,bhce->bhde", kd_n, v_new)
        return state, o_n

    s0 = jnp.zeros((q.shape[1], q.shape[2], q.shape[-1], v.shape[-1]), F32)
    _, o = lax.scan(step, s0, (u, w, a_qk, q_dec, k_dec, g_last))
    return from_chunks(o)


def chunk_gated_linear_attention(q, k, v, log_f):
    incl, _ = causal_masks()
    q, k, v = to_chunks(q), to_chunks(k), to_chunks(v)
    bcum = jnp.cumsum(to_chunks(log_f), axis=3)

    def step(state, xs):
        q_n, k_n, v_n, b_n = xs
        diff = b_n[:, :, :, None, :] - b_n[:, :, None, :, :]
        dec = jnp.exp(jnp.where(incl[:, :, None], diff, -jnp.inf))
        a = jnp.einsum("bhid,bhjd,bhijd->bhij", q_n, k_n, dec)
        b_last = b_n[:, :, -1:, :]
        o_n = (jnp.einsum("bhcd,bhde->bhce", q_n * jnp.exp(b_n), state)
               + jnp.einsum("bhij,bhje->bhie", a, v_n))
        state = (jnp.exp(b_last)[:, :, 0, :, None] * state
                 + jnp.einsum("bhcd,bhce->bhde", k_n * jnp.exp(b_last - b_n), v_n))
        return state, o_n

    s0 = jnp.zeros((q.shape[1], q.shape[2], q.shape[-1], v.shape[-1]), F32)
    _, o = lax.scan(step, s0, (q, k, v, bcum))
    return from_chunks(o)


def chunk_retention(q, k, v, log_gamma):
    incl, _ = causal_masks()
    pos = jnp.arange(CHUNK, dtype=F32)
    lg = log_gamma[:, None]
    intra = jnp.exp(jnp.where(incl, (pos[:, None] - pos[None, :]) * lg[:, :, None], -jnp.inf))
    q, k, v = to_chunks(q), to_chunks(k * q.shape[-1] ** -0.5), to_chunks(v)
    inner = jnp.einsum("nbhij,nbhje->nbhie", jnp.einsum("nbhid,nbhjd->nbhij", q, k) * intra, v)
    q_xi = q * jnp.exp((pos + 1.0) * lg)[:, :, None]
    k_zeta = k * jnp.exp((CHUNK - 1.0 - pos) * lg)[:, :, None]
    gamma_chunk = jnp.exp(CHUNK * log_gamma)[:, None, None]

    def step(state, xs):
        qx_n, kz_n, v_n = xs
        o_n = jnp.einsum("bhcd,bhde->bhce", qx_n, state)
        state = gamma_chunk * state + jnp.einsum("bhcd,bhce->bhde", kz_n, v_n)
        return state, o_n

    s0 = jnp.zeros((q.shape[1], q.shape[2], q.shape[-1], v.shape[-1]), F32)
    _, cross = lax.scan(step, s0, (q_xi, k_zeta, v))
    return from_chunks(inner + cross)


def gated_deltanet_branch(p, conv_w, a_log, dt_bias, norm_g):
    qkv = jnp.concatenate([p["gdn_q"], p["gdn_k"], p["gdn_v"]], axis=-1)
    qkv = jax.nn.silu(causal_depthwise_conv(qkv, conv_w).astype(F32))
    q, k, v = jnp.split(qkv, [N_HEADS * GDN_DK, 2 * N_HEADS * GDN_DK], axis=-1)
    q = l2_normalize(to_heads(q, GDN_DK)) * GDN_DK ** -0.5
    k = l2_normalize(to_heads(k, GDN_DK))
    v = to_heads(v, GDN_DV)
    beta = jax.nn.sigmoid(p["gdn_b"].astype(F32)).transpose(0, 2, 1)
    g = (-jnp.exp(a_log.astype(F32))
         * jax.nn.softplus(p["gdn_a"].astype(F32) + dt_bias.astype(F32))).transpose(0, 2, 1)
    o = gated_delta_rule_chunked(q, k, v, beta, g)
    return gate_heads(head_rms_norm(o, norm_g), p["gdn_gate"])


def hgrn2_branch(p, lower_bound, norm_g):
    q = to_heads(jax.nn.silu(p["hgrn_q"].astype(F32)), HGRN_DK) * HGRN_DK ** -0.5
    lb = lower_bound.astype(F32).reshape(N_HEADS, 1, HGRN_DK)
    f = lb + (1.0 - lb) * jax.nn.sigmoid(to_heads(p["hgrn_f"].astype(F32), HGRN_DK))
    i = to_heads(p["hgrn_i"].astype(F32), HGRN_DV)
    o = chunk_gated_linear_attention(q, 1.0 - f, i, jnp.log(f))
    return gate_heads(head_rms_norm(o, norm_g), p["hgrn_gate"])


def retention_branch(p, positions, norm_g, norm_b):
    q = rotary(to_heads(p["ret_q"].astype(F32), RET_DK), positions)
    k = rotary(to_heads(p["ret_k"].astype(F32), RET_DK), positions)
    v = to_heads(p["ret_v"].astype(F32), RET_DV)
    log_gamma = jnp.log1p(-jnp.exp(jnp.linspace(math.log(1.0 / 32.0), math.log(1.0 / 512.0), N_HEADS,
                                                dtype=F32)))
    o = chunk_retention(q, k, v, log_gamma)
    return gate_heads(head_group_norm(o, norm_g, norm_b), p["ret_gate"])


def gla_branch(p, gk_w2, gk_b, norm_g):
    q = to_heads(p["gla_q"].astype(F32), GLA_DK) * GLA_DK ** -0.5
    k = to_heads(p["gla_k"].astype(F32), GLA_DK)
    v = to_heads(p["gla_v"].astype(F32), GLA_DV)
    gk = jax.nn.log_sigmoid(p["gla_gk_lr"].astype(F32) @ gk_w2.astype(F32) + gk_b.astype(F32)) / GLA_GATE_NORM
    o = chunk_gated_linear_attention(q, k, v, to_heads(gk, GLA_DK))
    return gate_heads(head_rms_norm(o, norm_g), p["gla_gate"])


def token_mixing(u, positions, lower_bound, w_in, gdn_conv_w, gdn_a_log, gdn_dt_bias, gdn_norm_g,
                 hgrn_norm_g, ret_norm_g, ret_norm_b, gla_gk_w2, gla_gk_b, gla_norm_g,
                 w_branch, w_merge_gate, b_merge_gate, w_out):
    b, t, d = u.shape
    p = split_columns(u @ w_in)
    ys = jnp.stack([
        gated_deltanet_branch(p, gdn_conv_w, gdn_a_log, gdn_dt_bias, gdn_norm_g),
        hgrn2_branch(p, lower_bound, hgrn_norm_g),
        retention_branch(p, positions, ret_norm_g, ret_norm_b),
        gla_branch(p, gla_gk_w2, gla_gk_b, gla_norm_g),
    ], axis=2)
    branch = jnp.einsum("btmv,mvd->btmd", ys, w_branch)
    gates = jax.nn.sigmoid(u @ w_merge_gate + b_merge_gate).reshape(b, t, N_BRANCH, d)
    merged = jnp.sum(gates * branch, axis=2)
    return merged @ w_out


def swiglu(u, w_gate, w_up, w_down):
    return (jax.nn.silu(u @ w_gate) * (u @ w_up)) @ w_down


def setup_inputs(seed: int = 0) -> dict:
    key = jax.random.key(seed)
    ks = jax.random.split(key, 32)

    def nrm(k, shape, std):
        return jax.random.normal(k, shape, F32) * std

    def gain(k, shape):
        return 1.0 + 0.02 * jax.random.normal(k, shape, F32)

    dt = jnp.exp(jax.random.uniform(ks[7], (DEPTH, N_HEADS), F32, math.log(1e-3), math.log(1e-1)))
    return {
        "x": nrm(ks[0], (BATCH, SEQ, D_MODEL), 1.0),
        "c": nrm(ks[1], (BATCH, D_MODEL), 1.0),
        "positions": (jax.random.randint(ks[2], (BATCH, 1), 0, 1024, dtype=jnp.int32)
                      + jnp.arange(SEQ, dtype=jnp.int32)[None, :]),
        "ada_w": nrm(ks[3], (DEPTH, D_MODEL, 6 * D_MODEL), 0.5 * D_MODEL ** -0.5),
        "ada_b": nrm(ks[4], (DEPTH, 6 * D_MODEL), 0.01),
        "norm_mix_g": gain(ks[5], (DEPTH, D_MODEL)),
        "w_in": nrm(ks[6], (DEPTH, D_MODEL, D_IN), D_MODEL ** -0.5),
        "gdn_conv_w": nrm(ks[8], (DEPTH, CONV_W, 2 * N_HEADS * GDN_DK + N_HEADS * GDN_DV), CONV_W ** -0.5),
        "gdn_a_log": jnp.log(jax.random.uniform(ks[9], (DEPTH, N_HEADS), F32, 1.0, 16.0)),
        "gdn_dt_bias": dt + jnp.log(-jnp.expm1(-dt)),
        "gdn_norm_g": gain(ks[10], (DEPTH, GDN_DV)),
        "hgrn_lower_bounds": nrm(ks[11], (DEPTH, N_HEADS * HGRN_DK), 0.5),
        "hgrn_norm_g": gain(ks[12], (DEPTH, HGRN_DV)),
        "ret_norm_g": gain(ks[13], (DEPTH, RET_DV)),
        "ret_norm_b": nrm(ks[14], (DEPTH, RET_DV), 0.01),
        "gla_gk_w2": nrm(ks[15], (DEPTH, GLA_LOWRANK, N_HEADS * GLA_DK), GLA_LOWRANK ** -0.5),
        "gla_gk_b": nrm(ks[16], (DEPTH, N_HEADS * GLA_DK), 0.01),
        "gla_norm_g": gain(ks[17], (DEPTH, GLA_DV)),
        "w_branch": nrm(ks[18], (DEPTH, N_BRANCH, BRANCH_WIDTH, D_MODEL), BRANCH_WIDTH ** -0.5),
        "w_merge_gate": nrm(ks[19], (DEPTH, D_MODEL, N_BRANCH * D_MODEL), D_MODEL ** -0.5),
        "b_merge_gate": nrm(ks[20], (DEPTH, N_BRANCH * D_MODEL), 0.01),
        "w_out": nrm(ks[21], (DEPTH, D_MODEL, D_MODEL), D_MODEL ** -0.5),
        "norm_ffn_g": gain(ks[22], (DEPTH, D_MODEL)),
        "ffn_w_gate": nrm(ks[23], (DEPTH, D_MODEL, D_FF), D_MODEL ** -0.5),
        "ffn_w_up": nrm(ks[24], (DEPTH, D_MODEL, D_FF), D_MODEL ** -0.5),
        "ffn_w_down": nrm(ks[25], (DEPTH, D_FF, D_MODEL), D_FF ** -0.5),
        "final_norm_g": gain(ks[26], (D_MODEL,)),
    }


def reference(x, c, positions, ada_w, ada_b, norm_mix_g, w_in, gdn_conv_w, gdn_a_log, gdn_dt_bias,
              gdn_norm_g, hgrn_lower_bounds, hgrn_norm_g, ret_norm_g, ret_norm_b, gla_gk_w2, gla_gk_b,
              gla_norm_g, w_branch, w_merge_gate, b_merge_gate, w_out, norm_ffn_g, ffn_w_gate, ffn_w_up,
              ffn_w_down, final_norm_g):
    lbs = jax.nn.softmax(hgrn_lower_bounds.astype(F32), axis=0)
    lbs = jnp.cumsum(lbs, axis=0) - lbs[0]
    c_act = jax.nn.silu(c)
    h = x
    for l in range(DEPTH):
        mod = (c_act @ ada_w[l] + ada_b[l])[:, None, :]
        shift_m, scale_m, gate_m, shift_f, scale_f, gate_f = jnp.split(mod, 6, axis=-1)
        u = rms_norm(h, norm_mix_g[l]) * (1.0 + scale_m) + shift_m
        h = h + gate_m * token_mixing(
            u, positions, lbs[l], w_in[l], gdn_conv_w[l], gdn_a_log[l], gdn_dt_bias[l], gdn_norm_g[l],
            hgrn_norm_g[l], ret_norm_g[l], ret_norm_b[l], gla_gk_w2[l], gla_gk_b[l], gla_norm_g[l],
            w_branch[l], w_merge_gate[l], b_merge_gate[l], w_out[l])
        u = rms_norm(h, norm_ffn_g[l]) * (1.0 + scale_f) + shift_f
        h = h + gate_f * swiglu(u, ffn_w_gate[l], ffn_w_up[l], ffn_w_down[l])
    return rms_norm(h, final_norm_g)
```

```python
import functools
import math

import numpy as np
import jax
import jax.numpy as jnp
from jax import lax
from jax.experimental import pallas as pl
from jax.experimental.pallas import tpu as pltpu

F32 = jnp.float32
BF16 = jnp.bfloat16
HI = lax.Precision.HIGHEST

N_HEADS = 4
HEAD = 128
GROUP = N_HEADS * HEAD
GLA_DK = 64
GLA_LOWRANK = 16
GLA_GATE_NORM = 16.0
CONV_W = 4
EPS = 1e-6
ROPE_BASE = 10000.0
N_BRANCH = 4

GDN_CHUNK = 64
GLA_SUB = 16
RET_CHUNK = 256
SMALL = 128

LANE = 128
SUBLANE = 8
VMEM_CAP_BYTES = 60000 * 1024
INTERNAL_SCRATCH_BYTES = 12 * 1024 * 1024

G_GDN_Q, G_GDN_K, G_GDN_V, G_GDN_GATE = 0, 1, 2, 3
G_HGRN_Q, G_HGRN_F, G_HGRN_I, G_HGRN_GATE = 4, 5, 6, 7
G_RET_Q, G_RET_K, G_RET_V, G_RET_GATE = 8, 9, 10, 11
G_GLA_Q, G_GLA_K, G_GLA_V, G_GLA_GATE = 12, 13, 14, 15
N_GROUPS = 16

LOG_GAMMA = tuple(float(v) for v in np.log1p(-np.exp(np.linspace(math.log(1.0 / 32.0), math.log(1.0 / 512.0), N_HEADS))))

_NN = (((1,), (0,)), ((), ()))
_NT = (((1,), (1,)), ((), ()))
_TN = (((0,), (0,)), ((), ()))


def _mm(a, b, dims=_NN):
    return lax.dot_general(a.astype(BF16), b.astype(BF16), dims, preferred_element_type=F32)


def _mmh(a, b, dims=_NN):
    return lax.dot_general(a, b, dims, precision=HI, preferred_element_type=F32)


def _sigmoid(x):
    return 1.0 / (1.0 + jnp.exp(-x))


def _silu(x):
    return x * _sigmoid(x)


def _softplus(x):
    return jnp.maximum(x, 0.0) + jnp.log1p(jnp.exp(-jnp.abs(x)))


def _log_sigmoid(x):
    return -_softplus(-x)


def _rms(x, g):
    return x * lax.rsqrt(jnp.mean(x * x, axis=-1, keepdims=True) + EPS) * g


def _seg_cumsum(x, seg):
    row = lax.broadcasted_iota(jnp.int32, x.shape, 0) & (seg - 1)
    s = 1
    while s < seg:
        x = x + jnp.where(row >= s, pltpu.roll(x, s, axis=0), 0.0)
        s *= 2
    return x


def _pick(n, prefs):
    for p in prefs:
        if n % p == 0:
            return p
    return n


def _params(sem, *nbytes):
    need = int(sum(nbytes)) + INTERNAL_SCRATCH_BYTES
    return pltpu.CompilerParams(dimension_semantics=sem, vmem_limit_bytes=min(VMEM_CAP_BYTES, need))


def _mod_kernel(c_ref, w_ref, b_ref, o_ref):
    o_ref[...] = _mmh(_silu(c_ref[...]), w_ref[...]) + b_ref[...]


def _modulation(c, ada_w, ada_b):
    depth, d, n = ada_w.shape
    b = c.shape[0]
    rows = -(-b // SUBLANE) * SUBLANE
    c_pad = jnp.zeros((rows, d), F32).at[:b].set(c)
    tn = _pick(n, (1536, 1024, 512, 256, 128))
    out = pl.pallas_call(
        _mod_kernel,
        grid=(depth, n // tn),
        in_specs=[pl.BlockSpec((rows, d), lambda l, j: (0, 0)),
                  pl.BlockSpec((None, d, tn), lambda l, j: (l, 0, j)),
                  pl.BlockSpec((None, 1, tn), lambda l, j: (l, 0, j))],
        out_specs=pl.BlockSpec((None, rows, tn), lambda l, j: (l, 0, j)),
        out_shape=jax.ShapeDtypeStruct((depth, rows, n), F32),
        compiler_params=_params(("parallel", "parallel"), 2 * d * tn * 4, 4 * rows * tn * 4, 8 * rows * d * 4),
        name="adaln_modulation",
    )(c_pad, ada_w, ada_b.reshape(depth, 1, n))
    return out[:, :b]


def _inproj_kernel(h_ref, g_ref, sc_ref, sh_ref, w_ref, ws_ref, u_ref, z_ref, zs_ref):
    @pl.when(pl.program_id(1) == 0)
    def _():
        u = (_rms(h_ref[...], g_ref[...]) * (1.0 + sc_ref[...]) + sh_ref[...]).astype(BF16)
        u_ref[...] = u
        zs_ref[...] = lax.dot_general(u, ws_ref[...], _NN, preferred_element_type=F32)

    z_ref[...] = lax.dot_general(u_ref[...], w_ref[...], _NN, preferred_element_type=F32)


def _in_projection(h, gain, scale, shift, w_big, w_small, layer, seq):
    m, d = h.shape
    nb = w_big.shape[-1]
    tm = _pick(seq, (1024, 512, 256, 128))
    tn = _pick(nb, (1024, 512, 256, 128))
    per_b = seq // tm
    return pl.pallas_call(
        _inproj_kernel,
        grid=(m // tm, nb // tn),
        in_specs=[pl.BlockSpec((tm, d), lambda i, j: (i, 0)),
                  pl.BlockSpec((None, 1, d), lambda i, j: (layer, 0, 0)),
                  pl.BlockSpec((None, 1, d), lambda i, j: (i // per_b, 0, 0)),
                  pl.BlockSpec((None, 1, d), lambda i, j: (i // per_b, 0, 0)),
                  pl.BlockSpec((None, d, tn), lambda i, j: (layer, 0, j)),
                  pl.BlockSpec((None, d, SMALL), lambda i, j: (layer, 0, 0))],
        out_specs=[pl.BlockSpec((tm, d), lambda i, j: (i, 0)),
                   pl.BlockSpec((tm, tn), lambda i, j: (i, j)),
                   pl.BlockSpec((tm, SMALL), lambda i, j: (i, 0))],
        out_shape=[jax.ShapeDtypeStruct((m, d), BF16),
                   jax.ShapeDtypeStruct((m, nb), F32),
                   jax.ShapeDtypeStruct((m, SMALL), F32)],
        compiler_params=_params(("parallel", "arbitrary"), 2 * tm * d * 4, 2 * tm * d * 2, 2 * d * tn * 2,
                                2 * tm * tn * 4, 2 * d * SMALL * 2, 2 * tm * SMALL * 4, 2 * tm * d * 4),
        name="norm_in_projection",
    )(h, gain, scale, shift, w_big, w_small)


def _head_rms_gate(o, g, gate):
    return (o * lax.rsqrt(jnp.mean(o * o, axis=-1, keepdims=True) + EPS) * g * _silu(gate)).astype(BF16)


def _head_group_gate(o, g, b, gate):
    mu = jnp.mean(o, axis=-1, keepdims=True)
    oc = o - mu
    var = jnp.mean(oc * oc, axis=-1, keepdims=True)
    return ((oc * lax.rsqrt(var + EPS) * g + b) * _silu(gate)).astype(BF16)


def _zspec(tb, per_b, group):
    return pl.BlockSpec((tb, GROUP), lambda b, t: (b * per_b + t, group))


def _causal_conv(x, prev, w):
    acc = x * w[CONV_W - 1:CONV_W]
    rows = lax.broadcasted_iota(jnp.int32, prev.shape, 0)
    for s in range(1, CONV_W):
        xs = pltpu.roll(x, s, axis=0)
        head = jnp.where(rows < s, pltpu.roll(prev, s, axis=0), xs[0:SUBLANE])
        xs = jnp.concatenate([head, xs[SUBLANE:]], axis=0)
        acc = acc + xs * w[CONV_W - 1 - s:CONV_W - s]
    return acc


def _l2norm(x):
    return x * lax.rsqrt(jnp.sum(x * x, axis=-1, keepdims=True) + EPS)


def _unit_lower_inverse(lm, ii, jj):
    c = lm.shape[0]
    x = jnp.where(ii == jj, 1.0, 0.0) - jnp.where(((ii & 1) == 1) & (jj == ii - 1), lm, 0.0)
    k = 1
    while (1 << k) < c:
        same = (ii >> (k + 1)) == (jj >> (k + 1))
        sel = same & (((ii >> k) & 1) == 1) & (((jj >> k) & 1) == 0)
        x = x - _mmh(_mmh(x, jnp.where(sel, lm, 0.0)), x)
        k += 1
    return x


def _gdn_kernel(zq_ref, zk_ref, zv_ref, zg_ref, zs_ref, cw_ref, alog_ref, dt_ref, ng_ref, o_ref,
                prev_ref, q_s, k_s, v_s, gb_s, st_ref, *, tb):
    c = GDN_CHUNK

    @pl.when(pl.program_id(1) == 0)
    def _():
        prev_ref[...] = jnp.zeros(prev_ref.shape, F32)
        st_ref[...] = jnp.zeros(st_ref.shape, F32)

    cw = cw_ref[...]
    for idx, (src, dst) in enumerate(((zq_ref, q_s), (zk_ref, k_s), (zv_ref, v_s))):
        x = src[...]
        y = _silu(_causal_conv(x, prev_ref[idx], cw[:, idx * GROUP:(idx + 1) * GROUP]))
        prev_ref[idx] = x[tb - SUBLANE:tb]
        for h in range(N_HEADS):
            yh = y[:, h * HEAD:(h + 1) * HEAD]
            if idx == 0:
                yh = _l2norm(yh) * HEAD ** -0.5
            elif idx == 1:
                yh = _l2norm(yh)
            dst[h] = yh

    zs = zs_ref[...]
    lane = lax.broadcasted_iota(jnp.int32, (1, SMALL), 1)
    g = -jnp.exp(alog_ref[...]) * _softplus(zs + dt_ref[...])
    gcum = _seg_cumsum(jnp.where(lane < N_HEADS, g, 0.0), c)
    gb_s[...] = jnp.where(lane < N_HEADS, gcum, _sigmoid(zs))

    ii = lax.broadcasted_iota(jnp.int32, (c, c), 0)
    jj = lax.broadcasted_iota(jnp.int32, (c, c), 1)
    onehot = jnp.where(lax.broadcasted_iota(jnp.int32, (SUBLANE, SMALL), 0)
                       == lax.broadcasted_iota(jnp.int32, (SUBLANE, SMALL), 1), 1.0, 0.0)
    ng = ng_ref[...]

    def chunk(ci, carry):
        r = pl.ds(pl.multiple_of(ci * c, c), c)
        gbs = gb_s[r, :]
        grow = _mmh(onehot, gbs, _NT)
        for h in range(N_HEADS):
            q, k, v = q_s[h, r, :], k_s[h, r, :], v_s[h, r, :]
            gcol = gbs[:, h:h + 1]
            beta = gbs[:, N_HEADS + h:N_HEADS + h + 1]
            dec = jnp.exp(jnp.where(ii >= jj, gcol - grow[h:h + 1, :], -jnp.inf))
            kb = k * beta
            lm = jnp.where(ii > jj, _mmh(kb, k, _NT) * dec, 0.0)
            tinv = _unit_lower_inverse(lm, ii, jj)
            egc = jnp.exp(gcol)
            uw = _mmh(tinv, jnp.concatenate([v * beta, kb * egc], axis=1))
            u, w = uw[:, :HEAD], uw[:, HEAD:]
            aqk = _mm(q, k, _NT) * dec
            glast = gbs[c - 1:c, h:h + 1]
            st = st_ref[h]
            v_new = u - _mm(w, st)
            o = _mm(q * egc, st) + _mm(aqk, v_new)
            st_ref[h] = jnp.exp(glast) * st + _mm(k * jnp.exp(glast - gcol), v_new, _TN)
            o_ref[r, h * HEAD:(h + 1) * HEAD] = _head_rms_gate(o, ng, zg_ref[r, h * HEAD:(h + 1) * HEAD])
        return carry

    lax.fori_loop(0, tb // c, chunk, 0)


def _gdn_mixer(z, zs, conv_w, alog_row, dt_row, norm_g, layer, batch, seq):
    m = z.shape[0]
    tb = _pick(seq, (512, 256, 128, 64))
    per_b = seq // tb
    nconv = conv_w.shape[-1]
    row = lambda a: pl.BlockSpec((None, 1, a.shape[-1]), lambda b, t: (layer, 0, 0))
    return pl.pallas_call(
        functools.partial(_gdn_kernel, tb=tb),
        grid=(batch, per_b),
        in_specs=[_zspec(tb, per_b, G_GDN_Q), _zspec(tb, per_b, G_GDN_K), _zspec(tb, per_b, G_GDN_V),
                  _zspec(tb, per_b, G_GDN_GATE),
                  pl.BlockSpec((tb, SMALL), lambda b, t: (b * per_b + t, 0)),
                  pl.BlockSpec((None, CONV_W, nconv), lambda b, t: (layer, 0, 0)),
                  row(alog_row), row(dt_row), row(norm_g)],
        out_specs=pl.BlockSpec((tb, GROUP), lambda b, t: (b * per_b + t, 0)),
        out_shape=jax.ShapeDtypeStruct((m, GROUP), BF16),
        scratch_shapes=[pltpu.VMEM((3, SUBLANE, GROUP), F32),
                        pltpu.VMEM((N_HEADS, tb, HEAD), F32), pltpu.VMEM((N_HEADS, tb, HEAD), F32),
                        pltpu.VMEM((N_HEADS, tb, HEAD), F32), pltpu.VMEM((tb, SMALL), F32),
                        pltpu.VMEM((N_HEADS, HEAD, HEAD), F32)],
        compiler_params=_params(("parallel", "arbitrary"), 8 * tb * GROUP * 4, 2 * tb * GROUP * 2,
                                3 * tb * GROUP * 4, 8 * tb * GROUP * 4),
        name="gated_deltanet",
    )(z, z, z, z, zs, conv_w, alog_row, dt_row, norm_g)


def _gla_recurrence(q_s, k_s, v_s, b_s, st_ref, zg_ref, ng, o_ref, tb):
    n = GLA_SUB
    rows = lax.broadcasted_iota(jnp.int32, (n, HEAD), 0)

    def sub(si, carry):
        r = pl.ds(pl.multiple_of(si * n, n), n)
        for h in range(N_HEADS):
            q, k, v, b = q_s[h, r, :], k_s[h, r, :], v_s[h, r, :], b_s[h, r, :]
            st = st_ref[h]
            o = _mm(q * jnp.exp(b), st, _NT)
            for j in range(n):
                e = jnp.exp(jnp.where(rows >= j, b - b[j:j + 1, :], -jnp.inf))
                a = jnp.sum(q * e * k[j:j + 1, :], axis=-1, keepdims=True)
                o = o + a * v[j:j + 1, :]
            blast = b[n - 1:n, :]
            st_ref[h] = st * jnp.exp(blast) + _mm(v, k * jnp.exp(blast - b), _TN)
            o_ref[r, h * HEAD:(h + 1) * HEAD] = _head_rms_gate(o, ng, zg_ref[r, h * HEAD:(h + 1) * HEAD])
        return carry

    lax.fori_loop(0, tb // n, sub, 0)


def _hgrn_kernel(zq_ref, zf_ref, zi_ref, zg_ref, lb_ref, ng_ref, o_ref, q_s, k_s, v_s, b_s, st_ref, *, tb, layer):
    @pl.when(pl.program_id(1) == 0)
    def _():
        st_ref[...] = jnp.zeros(st_ref.shape, F32)

    x = lb_ref[...]
    e = jnp.exp(x - jnp.max(x, axis=0, keepdims=True))
    sm = e / jnp.sum(e, axis=0, keepdims=True)
    lb_all = jnp.zeros((1, GROUP), F32)
    for j in range(1, layer + 1):
        lb_all = lb_all + sm[j:j + 1, :]
    for h in range(N_HEADS):
        sl = slice(h * HEAD, (h + 1) * HEAD)
        lb = lb_all[:, sl]
        f = lb + (1.0 - lb) * _sigmoid(zf_ref[:, sl])
        q_s[h] = _silu(zq_ref[:, sl]) * HEAD ** -0.5
        k_s[h] = 1.0 - f
        v_s[h] = zi_ref[:, sl]
        b_s[h] = _seg_cumsum(jnp.log(f), GLA_SUB)
    _gla_recurrence(q_s, k_s, v_s, b_s, st_ref, zg_ref, ng_ref[...], o_ref, tb)


def _gla_kernel(zq_ref, zk_ref, zv_ref, zg_ref, zs_ref, w2_ref, gkb_ref, ng_ref, o_ref,
                q_s, k_s, v_s, b_s, st_ref, *, tb):
    @pl.when(pl.program_id(1) == 0)
    def _():
        st_ref[...] = jnp.zeros(st_ref.shape, F32)

    zs = zs_ref[...]
    lane = lax.broadcasted_iota(jnp.int32, (1, HEAD), 1)
    for h in range(N_HEADS):
        sl = slice(h * HEAD, (h + 1) * HEAD)
        gk = _log_sigmoid(_mmh(zs, w2_ref[h]) + gkb_ref[h]) / GLA_GATE_NORM
        q_s[h] = zq_ref[:, sl] * GLA_DK ** -0.5
        k_s[h] = zk_ref[:, sl]
        v_s[h] = zv_ref[:, sl]
        b_s[h] = _seg_cumsum(jnp.where(lane < GLA_DK, gk, 0.0), GLA_SUB)
    _gla_recurrence(q_s, k_s, v_s, b_s, st_ref, zg_ref, ng_ref[...], o_ref, tb)


def _gla_scratch(tb):
    return [pltpu.VMEM((N_HEADS, tb, HEAD), F32) for _ in range(4)] + [pltpu.VMEM((N_HEADS, HEAD, HEAD), F32)]


def _hgrn_mixer(z, lower_bounds, norm_g, layer, batch, seq):
    m = z.shape[0]
    tb = _pick(seq, (512, 256, 128, 64))
    per_b = seq // tb
    depth = lower_bounds.shape[0]
    return pl.pallas_call(
        functools.partial(_hgrn_kernel, tb=tb, layer=layer),
        grid=(batch, per_b),
        in_specs=[_zspec(tb, per_b, G_HGRN_Q), _zspec(tb, per_b, G_HGRN_F), _zspec(tb, per_b, G_HGRN_I),
                  _zspec(tb, per_b, G_HGRN_GATE),
                  pl.BlockSpec((depth, GROUP), lambda b, t: (0, 0)),
                  pl.BlockSpec((None, 1, HEAD), lambda b, t: (layer, 0, 0))],
        out_specs=pl.BlockSpec((tb, GROUP), lambda b, t: (b * per_b + t, 0)),
        out_shape=jax.ShapeDtypeStruct((m, GROUP), BF16),
        scratch_shapes=_gla_scratch(tb),
        compiler_params=_params(("parallel", "arbitrary"), 8 * tb * GROUP * 4, 2 * tb * GROUP * 2,
                                4 * tb * GROUP * 4, 8 * tb * GROUP * 4),
        name="hgrn2",
    )(z, z, z, z, lower_bounds, norm_g)


def _gla_mixer(z, zs, w2_pad, gkb_pad, norm_g, layer, batch, seq):
    m = z.shape[0]
    tb = _pick(seq, (512, 256, 128, 64))
    per_b = seq // tb
    return pl.pallas_call(
        functools.partial(_gla_kernel, tb=tb),
        grid=(batch, per_b),
        in_specs=[_zspec(tb, per_b, G_GLA_Q), _zspec(tb, per_b, G_GLA_K), _zspec(tb, per_b, G_GLA_V),
                  _zspec(tb, per_b, G_GLA_GATE),
                  pl.BlockSpec((tb, SMALL), lambda b, t: (b * per_b + t, 0)),
                  pl.BlockSpec((None, N_HEADS, SMALL, HEAD), lambda b, t: (layer, 0, 0, 0)),
                  pl.BlockSpec((None, N_HEADS, 1, HEAD), lambda b, t: (layer, 0, 0, 0)),
                  pl.BlockSpec((None, 1, HEAD), lambda b, t: (layer, 0, 0))],
        out_specs=pl.BlockSpec((tb, GROUP), lambda b, t: (b * per_b + t, 0)),
        out_shape=jax.ShapeDtypeStruct((m, GROUP), BF16),
        scratch_shapes=_gla_scratch(tb),
        compiler_params=_params(("parallel", "arbitrary"), 8 * tb * GROUP * 4, 2 * tb * GROUP * 2,
                                4 * tb * GROUP * 4, 8 * tb * GROUP * 4),
        name="gla",
    )(z, z, z, z, zs, w2_pad, gkb_pad, norm_g)


def _rope_kernel(pos_ref, inv_ref, cos_ref, sin_ref):
    ang = pos_ref[...].astype(F32) * inv_ref[...]
    lane = lax.broadcasted_iota(jnp.int32, ang.shape, 1)
    cos_ref[...] = jnp.cos(ang)
    sin_ref[...] = jnp.where(lane < HEAD // 2, -jnp.sin(ang), jnp.sin(ang))


def _rope_tables(positions):
    m = positions.size
    half = HEAD // 2
    inv = (ROPE_BASE ** (-np.arange(half, dtype=np.float64) / half)).astype(np.float32)
    inv2 = jnp.asarray(np.concatenate([inv, inv])[None, :])
    tb = _pick(m, (1024, 512, 256, 128, 64))
    return pl.pallas_call(
        _rope_kernel,
        grid=(m // tb,),
        in_specs=[pl.BlockSpec((tb, 1), lambda i: (i, 0)), pl.BlockSpec((1, HEAD), lambda i: (0, 0))],
        out_specs=[pl.BlockSpec((tb, HEAD), lambda i: (i, 0)), pl.BlockSpec((tb, HEAD), lambda i: (i, 0))],
        out_shape=[jax.ShapeDtypeStruct((m, HEAD), F32), jax.ShapeDtypeStruct((m, HEAD), F32)],
        compiler_params=_params(("parallel",), 2 * tb * LANE * 4, 4 * tb * HEAD * 4, 4 * tb * HEAD * 4),
        name="rope_tables",
    )(positions.reshape(m, 1), inv2)


def _ret_kernel(zq_ref, zk_ref, zv_ref, zg_ref, cos_ref, sin_ref, ng_ref, nb_ref, o_ref, st_ref, *, tb):
    c = RET_CHUNK

    @pl.when(pl.program_id(1) == 0)
    def _():
        st_ref[...] = jnp.zeros(st_ref.shape, F32)

    ii = lax.broadcasted_iota(jnp.int32, (c, c), 0)
    jj = lax.broadcasted_iota(jnp.int32, (c, c), 1)
    pos = lax.broadcasted_iota(jnp.int32, (c, 1), 0).astype(F32)
    ng, nb = ng_ref[...], nb_ref[...]
    for h in range(N_HEADS):
        sl = slice(h * HEAD, (h + 1) * HEAD)
        lg = LOG_GAMMA[h]
        intra = jnp.where(ii >= jj, jnp.exp((ii - jj).astype(F32) * lg), 0.0)
        xi = jnp.exp((pos + 1.0) * lg)
        zeta = jnp.exp((c - 1.0 - pos) * lg)
        for ci in range(tb // c):
            r = slice(ci * c, (ci + 1) * c)
            cos, sin = cos_ref[r, :], sin_ref[r, :]
            xq, xk = zq_ref[r, sl], zk_ref[r, sl]
            q = xq * cos + pltpu.roll(xq, HEAD // 2, axis=1) * sin
            k = (xk * cos + pltpu.roll(xk, HEAD // 2, axis=1) * sin) * HEAD ** -0.5
            v = zv_ref[r, sl]
            st = st_ref[h]
            o = _mm(_mm(q, k, _NT) * intra, v) + _mm(q * xi, st)
            st_ref[h] = math.exp(c * lg) * st + _mm(k * zeta, v, _TN)
            o_ref[r, sl] = _head_group_gate(o, ng, nb, zg_ref[r, sl])


def _ret_mixer(z, cos2, sin2, norm_g, norm_b, layer, batch, seq):
    m = z.shape[0]
    tb = _pick(seq, (512, 256))
    per_b = seq // tb
    rowspec = pl.BlockSpec((tb, HEAD), lambda b, t: (b * per_b + t, 0))
    prm = pl.BlockSpec((None, 1, HEAD), lambda b, t: (layer, 0, 0))
    return pl.pallas_call(
        functools.partial(_ret_kernel, tb=tb),
        grid=(batch, per_b),
        in_specs=[_zspec(tb, per_b, G_RET_Q), _zspec(tb, per_b, G_RET_K), _zspec(tb, per_b, G_RET_V),
                  _zspec(tb, per_b, G_RET_GATE), rowspec, rowspec, prm, prm],
        out_specs=pl.BlockSpec((tb, GROUP), lambda b, t: (b * per_b + t, 0)),
        out_shape=jax.ShapeDtypeStruct((m, GROUP), BF16),
        scratch_shapes=[pltpu.VMEM((N_HEADS, HEAD, HEAD), F32)],
        compiler_params=_params(("parallel", "arbitrary"), 8 * tb * GROUP * 4, 2 * tb * GROUP * 2,
                                4 * tb * HEAD * 4, 16 * RET_CHUNK * RET_CHUNK * 4),
        name="retention",
    )(z, z, z, z, cos2, sin2, norm_g, norm_b)


def _merge_kernel(u_ref, y0, y1, y2, y3, g0, g1, g2, g3, b0, b1, b2, b3, wb_ref, o_ref):
    u = u_ref[...]
    acc = None
    for m, (y, wg, bg) in enumerate(((y0, g0, b0), (y1, g1, b1), (y2, g2, b2), (y3, g3, b3))):
        gate = _sigmoid(lax.dot_general(u, wg[...], _NN, preferred_element_type=F32) + bg[...])
        term = gate * lax.dot_general(y[...], wb_ref[m], _NN, preferred_element_type=F32)
        acc = term if acc is None else acc + term
    o_ref[...] = acc.astype(BF16)


def _merge(u, ys, w_gate, b_gate, w_branch, layer):
    m, d = u.shape
    tm = _pick(m, (1024, 512, 256, 128))
    td = _pick(d, (512, 256, 128))
    nd = d // td
    yspec = pl.BlockSpec((tm, GROUP), lambda i, j: (i, 0))
    gspecs = [pl.BlockSpec((None, d, td), functools.partial(lambda i, j, br: (layer, 0, br * nd + j), br=br))
              for br in range(N_BRANCH)]
    bspecs = [pl.BlockSpec((None, 1, td), functools.partial(lambda i, j, br: (layer, 0, br * nd + j), br=br))
              for br in range(N_BRANCH)]
    return pl.pallas_call(
        _merge_kernel,
        grid=(m // tm, nd),
        in_specs=[pl.BlockSpec((tm, d), lambda i, j: (i, 0))] + [yspec] * N_BRANCH + gspecs + bspecs
                 + [pl.BlockSpec((None, N_BRANCH, GROUP, td), lambda i, j: (layer, 0, 0, j))],
        out_specs=pl.BlockSpec((tm, td), lambda i, j: (i, j)),
        out_shape=jax.ShapeDtypeStruct((m, d), BF16),
        compiler_params=_params(("parallel", "arbitrary"), 2 * tm * d * 2, 2 * N_BRANCH * tm * GROUP * 2,
                                2 * N_BRANCH * d * td * 2, 2 * N_BRANCH * GROUP * td * 2, 2 * tm * td * 2,
                                6 * tm * td * 4),
        name="branch_merge",
    )(u, *ys, *([w_gate] * N_BRANCH), *([b_gate] * N_BRANCH), w_branch)


def _outproj_kernel(x_ref, w_ref, h_ref, gate_ref, o_ref):
    o_ref[...] = h_ref[...] + gate_ref[...] * lax.dot_general(x_ref[...], w_ref[...], _NN, preferred_element_type=F32)


def _out_projection(x, w_out, h, gate, layer, seq):
    m, d = h.shape
    tm = _pick(seq, (512, 256, 128))
    per_b = seq // tm
    return pl.pallas_call(
        _outproj_kernel,
        grid=(m // tm,),
        in_specs=[pl.BlockSpec((tm, d), lambda i: (i, 0)),
                  pl.BlockSpec((None, d, d), lambda i: (layer, 0, 0)),
                  pl.BlockSpec((tm, d), lambda i: (i, 0)),
                  pl.BlockSpec((None, 1, d), lambda i: (i // per_b, 0, 0))],
        out_specs=pl.BlockSpec((tm, d), lambda i: (i, 0)),
        out_shape=jax.ShapeDtypeStruct((m, d), F32),
        compiler_params=_params(("parallel",), 2 * tm * d * 2, 2 * d * d * 2, 4 * tm * d * 4, 2 * tm * d * 4),
        name="out_projection",
    )(x, w_out, h, gate)


def _ffn_kernel(h_ref, g_ref, sc_ref, sh_ref, gt_ref, wg_ref, wu_ref, wd_ref, o_ref, u_s, acc_s):
    f = pl.program_id(1)

    @pl.when(f == 0)
    def _():
        u_s[...] = (_rms(h_ref[...], g_ref[...]) * (1.0 + sc_ref[...]) + sh_ref[...]).astype(BF16)
        acc_s[...] = jnp.zeros(acc_s.shape, F32)

    u = u_s[...]
    a = lax.dot_general(u, wg_ref[...], _NN, preferred_element_type=F32)
    b = lax.dot_general(u, wu_ref[...], _NN, preferred_element_type=F32)
    acc_s[...] += lax.dot_general((_silu(a) * b).astype(BF16), wd_ref[...], _NN, preferred_element_type=F32)

    @pl.when(f == pl.num_programs(1) - 1)
    def _():
        o_ref[...] = h_ref[...] + gt_ref[...] * acc_s[...]


def _ffn(h, gain, scale, shift, gate, w_gate, w_up, w_down, layer, seq):
    m, d = h.shape
    dff = w_gate.shape[-1]
    tm = _pick(seq, (512, 256, 128))
    tf = _pick(dff, (512, 256, 128))
    per_b = seq // tm
    mod = pl.BlockSpec((None, 1, d), lambda i, f: (i // per_b, 0, 0))
    return pl.pallas_call(
        _ffn_kernel,
        grid=(m // tm, dff // tf),
        in_specs=[pl.BlockSpec((tm, d), lambda i, f: (i, 0)),
                  pl.BlockSpec((None, 1, d), lambda i, f: (layer, 0, 0)),
                  mod, mod, mod,
                  pl.BlockSpec((None, d, tf), lambda i, f: (layer, 0, f)),
                  pl.BlockSpec((None, d, tf), lambda i, f: (layer, 0, f)),
                  pl.BlockSpec((None, tf, d), lambda i, f: (layer, f, 0))],
        out_specs=pl.BlockSpec((tm, d), lambda i, f: (i, 0)),
        out_shape=jax.ShapeDtypeStruct((m, d), F32),
        scratch_shapes=[pltpu.VMEM((tm, d), BF16), pltpu.VMEM((tm, d), F32)],
        compiler_params=_params(("parallel", "arbitrary"), 4 * tm * d * 4, 6 * d * tf * 2, tm * d * 6,
                                4 * tm * tf * 4, tm * d * 4),
        name="swiglu_ffn",
    )(h, gain, scale, shift, gate, w_gate, w_up, w_down)


def _final_norm_kernel(x_ref, g_ref, o_ref):
    o_ref[...] = _rms(x_ref[...], g_ref[...])


def _final_norm(h, gain):
    m, d = h.shape
    tm = _pick(m, (512, 256, 128))
    return pl.pallas_call(
        _final_norm_kernel,
        grid=(m // tm,),
        in_specs=[pl.BlockSpec((tm, d), lambda i: (i, 0)), pl.BlockSpec((1, d), lambda i: (0, 0))],
        out_specs=pl.BlockSpec((tm, d), lambda i: (i, 0)),
        out_shape=jax.ShapeDtypeStruct((m, d), F32),
        compiler_params=_params(("parallel",), 4 * tm * d * 4, 2 * tm * d * 4),
        name="final_norm",
    )(h, gain.reshape(1, d))


def _split_offsets():
    widths = (("gdn_q", GROUP), ("gdn_k", GROUP), ("gdn_v", GROUP), ("gdn_a", N_HEADS), ("gdn_b", N_HEADS),
              ("gdn_gate", GROUP), ("hgrn_q", GROUP), ("hgrn_f", GROUP), ("hgrn_i", GROUP), ("hgrn_gate", GROUP),
              ("ret_q", GROUP), ("ret_k", GROUP), ("ret_v", GROUP), ("ret_gate", GROUP),
              ("gla_q", N_HEADS * GLA_DK), ("gla_k", N_HEADS * GLA_DK), ("gla_v", GROUP),
              ("gla_gk_lr", GLA_LOWRANK), ("gla_gate", GROUP))
    out, off = {}, 0
    for name, w in widths:
        out[name] = (off, w)
        off += w
    return out, off


def _permute_w_in(w_in):
    offs, total = _split_offsets()
    assert w_in.shape[-1] == total
    depth, d, _ = w_in.shape

    def cols(name):
        o, w = offs[name]
        return w_in[:, :, o:o + w]

    def pad_heads(name):
        x = cols(name).reshape(depth, d, N_HEADS, GLA_DK)
        return jnp.pad(x, ((0, 0), (0, 0), (0, 0), (0, HEAD - GLA_DK))).reshape(depth, d, GROUP)

    big = [cols("gdn_q"), cols("gdn_k"), cols("gdn_v"), cols("gdn_gate"),
           cols("hgrn_q"), cols("hgrn_f"), cols("hgrn_i"), cols("hgrn_gate"),
           cols("ret_q"), cols("ret_k"), cols("ret_v"), cols("ret_gate"),
           pad_heads("gla_q"), pad_heads("gla_k"), cols("gla_v"), cols("gla_gate")]
    small = jnp.concatenate([cols("gdn_a"), cols("gdn_b"), cols("gla_gk_lr")], axis=-1)
    small = jnp.pad(small, ((0, 0), (0, 0), (0, SMALL - small.shape[-1])))
    return jnp.concatenate(big, axis=-1).astype(BF16), small.astype(BF16)


def _lane_row(x, width=SMALL):
    return jnp.pad(x.astype(F32), ((0, 0), (0, width - x.shape[-1])))[:, None, :]


def kernel(x, c, positions, ada_w, ada_b, norm_mix_g, w_in, gdn_conv_w, gdn_a_log, gdn_dt_bias, gdn_norm_g, hgrn_lower_bounds, hgrn_norm_g, ret_norm_g, ret_norm_b, gla_gk_w2, gla_gk_b, gla_norm_g, w_branch, w_merge_gate, b_merge_gate, w_out, norm_ffn_g, ffn_w_gate, ffn_w_up, ffn_w_down, final_norm_g):
    batch, seq, d = x.shape
    depth = w_in.shape[0]
    m = batch * seq
    assert seq % max(RET_CHUNK, GDN_CHUNK) == 0 and d % LANE == 0

    w_big, w_small = _permute_w_in(w_in)
    w_mg = w_merge_gate.astype(BF16)
    b_mg = b_merge_gate.reshape(depth, 1, N_BRANCH * d)
    w_br = w_branch.astype(BF16)
    w_o = w_out.astype(BF16)
    w_fg, w_fu, w_fd = ffn_w_gate.astype(BF16), ffn_w_up.astype(BF16), ffn_w_down.astype(BF16)
    alog_row, dt_row = _lane_row(gdn_a_log), _lane_row(gdn_dt_bias)
    w2 = gla_gk_w2.reshape(depth, GLA_LOWRANK, N_HEADS, GLA_DK).transpose(0, 2, 1, 3)
    w2_pad = jnp.zeros((depth, N_HEADS, SMALL, HEAD), F32).at[:, :, 2 * N_HEADS:2 * N_HEADS + GLA_LOWRANK, :GLA_DK].set(w2)
    gkb_pad = jnp.pad(gla_gk_b.reshape(depth, N_HEADS, 1, GLA_DK), ((0, 0), (0, 0), (0, 0), (0, HEAD - GLA_DK)))
    row3 = lambda a: a.reshape(depth, 1, a.shape[-1])

    mod = _modulation(c, ada_w, ada_b)
    cos2, sin2 = _rope_tables(positions)

    h = x.reshape(m, d)
    for l in range(depth):
        shift_m, scale_m, gate_m, shift_f, scale_f, gate_f = [mod[l, :, i * d:(i + 1) * d][:, None, :] for i in range(6)]
        u, z, zs = _in_projection(h, row3(norm_mix_g), scale_m, shift_m, w_big, w_small, l, seq)
        ys = (_gdn_mixer(z, zs, gdn_conv_w, alog_row, dt_row, row3(gdn_norm_g), l, batch, seq),
              _hgrn_mixer(z, hgrn_lower_bounds, row3(hgrn_norm_g), l, batch, seq),
              _ret_mixer(z, cos2, sin2, row3(ret_norm_g), row3(ret_norm_b), l, batch, seq),
              _gla_mixer(z, zs, w2_pad, gkb_pad, row3(gla_norm_g), l, batch, seq))
        merged = _merge(u, ys, w_mg, b_mg, w_br, l)
        h = _out_projection(merged, w_o, h, gate_m, l, seq)
        h = _ffn(h, row3(norm_ffn_g), scale_f, shift_f, gate_f, w_fg, w_fu, w_fd, l, seq)
    return _final_norm(h, final_norm_g).reshape(batch, seq, d)
```

```python
import functools
import math

import numpy as np
import jax
import jax.numpy as jnp
from jax import lax
from jax.experimental import pallas as pl
from jax.experimental.pallas import tpu as pltpu

F32 = jnp.float32
BF16 = jnp.bfloat16
HI = lax.Precision.HIGHEST

N_HEADS = 4
HEAD = 128
GROUP = N_HEADS * HEAD
GLA_DK = 64
GLA_LOWRANK = 16
GLA_GATE_NORM = 16.0
CONV_W = 4
EPS = 1e-6
ROPE_BASE = 10000.0
N_BRANCH = 4

GDN_CHUNK = 64
GLA_CHUNK = 64
RET_CHUNK = 256
SMALL = 128

LANE = 128
SUBLANE = 8
VMEM_CAP_BYTES = 60000 * 1024
INTERNAL_SCRATCH_BYTES = 12 * 1024 * 1024

G_GDN_Q, G_GDN_K, G_GDN_V, G_GDN_GATE = 0, 1, 2, 3
G_HGRN_Q, G_HGRN_F, G_HGRN_I, G_HGRN_GATE = 4, 5, 6, 7
G_RET_Q, G_RET_K, G_RET_V, G_RET_GATE = 8, 9, 10, 11
G_GLA_Q, G_GLA_K, G_GLA_V, G_GLA_GATE = 12, 13, 14, 15
N_GROUPS = 16

LOG_GAMMA = tuple(float(v) for v in np.log1p(-np.exp(np.linspace(math.log(1.0 / 32.0), math.log(1.0 / 512.0), N_HEADS))))

_NN = (((1,), (0,)), ((), ()))
_NT = (((1,), (1,)), ((), ()))
_TN = (((0,), (0,)), ((), ()))


def _mm(a, b, dims=_NN):
    return lax.dot_general(a.astype(BF16), b.astype(BF16), dims, preferred_element_type=F32)


def _mmh(a, b, dims=_NN):
    return lax.dot_general(a, b, dims, precision=HI, preferred_element_type=F32)


def _sigmoid(x):
    return 1.0 / (1.0 + jnp.exp(-x))


def _silu(x):
    return x * _sigmoid(x)


def _softplus(x):
    return jnp.maximum(x, 0.0) + jnp.log1p(jnp.exp(-jnp.abs(x)))


def _log_sigmoid(x):
    return -_softplus(-x)


def _rms(x, g):
    return x * lax.rsqrt(jnp.mean(x * x, axis=-1, keepdims=True) + EPS) * g


def _seg_cumsum(x, seg):
    row = lax.broadcasted_iota(jnp.int32, x.shape, 0) & (seg - 1)
    s = 1
    while s < seg:
        x = x + jnp.where(row >= s, pltpu.roll(x, s, axis=0), 0.0)
        s *= 2
    return x


def _pick(n, prefs):
    for p in prefs:
        if n % p == 0:
            return p
    return n


def _params(sem, *nbytes):
    need = int(sum(nbytes)) + INTERNAL_SCRATCH_BYTES
    return pltpu.CompilerParams(dimension_semantics=sem, vmem_limit_bytes=min(VMEM_CAP_BYTES, need))


def _mod_kernel(c_ref, w_ref, b_ref, o_ref):
    o_ref[...] = _mmh(_silu(c_ref[...]), w_ref[...]) + b_ref[...]


def _modulation(c, ada_w, ada_b):
    depth, d, n = ada_w.shape
    b = c.shape[0]
    rows = -(-b // SUBLANE) * SUBLANE
    c_pad = jnp.zeros((rows, d), F32).at[:b].set(c)
    tn = _pick(n, (1536, 1024, 512, 256, 128))
    out = pl.pallas_call(
        _mod_kernel,
        grid=(depth, n // tn),
        in_specs=[pl.BlockSpec((rows, d), lambda l, j: (0, 0)),
                  pl.BlockSpec((None, d, tn), lambda l, j: (l, 0, j)),
                  pl.BlockSpec((None, 1, tn), lambda l, j: (l, 0, j))],
        out_specs=pl.BlockSpec((None, rows, tn), lambda l, j: (l, 0, j)),
        out_shape=jax.ShapeDtypeStruct((depth, rows, n), F32),
        compiler_params=_params(("parallel", "parallel"), 2 * d * tn * 4, 4 * rows * tn * 4, 8 * rows * d * 4),
        name="adaln_modulation",
    )(c_pad, ada_w, ada_b.reshape(depth, 1, n))
    return out[:, :b]


def _inproj_kernel(h_ref, g_ref, sc_ref, sh_ref, w_ref, ws_ref, u_ref, z_ref, zs_ref):
    @pl.when(pl.program_id(1) == 0)
    def _():
        u = (_rms(h_ref[...], g_ref[...]) * (1.0 + sc_ref[...]) + sh_ref[...]).astype(BF16)
        u_ref[...] = u
        zs_ref[...] = lax.dot_general(u, ws_ref[...], _NN, preferred_element_type=F32)

    z_ref[...] = lax.dot_general(u_ref[...], w_ref[...], _NN, preferred_element_type=F32)


def _in_projection(h, gain, scale, shift, w_big, w_small, layer, seq):
    m, d = h.shape
    nb = w_big.shape[-1]
    tm = _pick(seq, (1024, 512, 256, 128))
    tn = _pick(nb, (1024, 512, 256, 128))
    per_b = seq // tm
    return pl.pallas_call(
        _inproj_kernel,
        grid=(m // tm, nb // tn),
        in_specs=[pl.BlockSpec((tm, d), lambda i, j: (i, 0)),
                  pl.BlockSpec((None, 1, d), lambda i, j: (layer, 0, 0)),
                  pl.BlockSpec((None, 1, d), lambda i, j: (i // per_b, 0, 0)),
                  pl.BlockSpec((None, 1, d), lambda i, j: (i // per_b, 0, 0)),
                  pl.BlockSpec((None, d, tn), lambda i, j: (layer, 0, j)),
                  pl.BlockSpec((None, d, SMALL), lambda i, j: (layer, 0, 0))],
        out_specs=[pl.BlockSpec((tm, d), lambda i, j: (i, 0)),
                   pl.BlockSpec((tm, tn), lambda i, j: (i, j)),
                   pl.BlockSpec((tm, SMALL), lambda i, j: (i, 0))],
        out_shape=[jax.ShapeDtypeStruct((m, d), BF16),
                   jax.ShapeDtypeStruct((m, nb), F32),
                   jax.ShapeDtypeStruct((m, SMALL), F32)],
        compiler_params=_params(("parallel", "arbitrary"), 2 * tm * d * 4, 2 * tm * d * 2, 2 * d * tn * 2,
                                2 * tm * tn * 4, 2 * d * SMALL * 2, 2 * tm * SMALL * 4, 2 * tm * d * 4),
        name="norm_in_projection",
    )(h, gain, scale, shift, w_big, w_small)


def _head_rms_gate(o, g, gate):
    return (o * lax.rsqrt(jnp.mean(o * o, axis=-1, keepdims=True) + EPS) * g * _silu(gate)).astype(BF16)


def _head_group_gate(o, g, b, gate):
    mu = jnp.mean(o, axis=-1, keepdims=True)
    oc = o - mu
    var = jnp.mean(oc * oc, axis=-1, keepdims=True)
    return ((oc * lax.rsqrt(var + EPS) * g + b) * _silu(gate)).astype(BF16)


def _zspec(tb, per_b, group):
    return pl.BlockSpec((tb, GROUP), lambda b, t: (b * per_b + t, group))


def _causal_conv(x, prev, w):
    acc = x * w[CONV_W - 1:CONV_W]
    rows = lax.broadcasted_iota(jnp.int32, prev.shape, 0)
    for s in range(1, CONV_W):
        xs = pltpu.roll(x, s, axis=0)
        head = jnp.where(rows < s, pltpu.roll(prev, s, axis=0), xs[0:SUBLANE])
        xs = jnp.concatenate([head, xs[SUBLANE:]], axis=0)
        acc = acc + xs * w[CONV_W - 1 - s:CONV_W - s]
    return acc


def _l2norm(x):
    return x * lax.rsqrt(jnp.sum(x * x, axis=-1, keepdims=True) + EPS)


def _split_bf16(a):
    hi = a.astype(BF16)
    return hi, (a - hi.astype(F32)).astype(BF16)


_BNN = (((2,), (1,)), ((0,), (0,)))
_BNT = (((2,), (2,)), ((0,), (0,)))


def _bdot(a, b, dims):
    return lax.dot_general(a, b, dims, preferred_element_type=F32)


def _bmm3(a, b, dims=_BNN):
    (ah, al), (bh, bl) = a, b
    return _bdot(ah, bh, dims) + _bdot(ah, bl, dims) + _bdot(al, bh, dims)


def _unit_lower_inverse(lm, ii, jj):
    c = lm.shape[-1]
    x = jnp.where(ii == jj, 1.0, 0.0) - jnp.where(((ii & 1) == 1) & (jj == ii - 1), lm, 0.0)
    k = 1
    while (1 << k) < c:
        same = (ii >> (k + 1)) == (jj >> (k + 1))
        sel = same & (((ii >> k) & 1) == 1) & (((jj >> k) & 1) == 0)
        xs = _split_bf16(x)
        xc = _bmm3(xs, _split_bf16(jnp.where(sel, lm, 0.0)))
        x = x - _bmm3(_split_bf16(xc), xs)
        k += 1
    return x


def _gdn_kernel(zq_ref, zk_ref, zv_ref, zg_ref, zs_ref, cw_ref, alog_ref, dt_ref, ng_ref, o_ref,
                prev_ref, q_s, k_s, v_s, gb_s, u_s, w_s, a_s, st_ref, *, tb):
    c = GDN_CHUNK
    nc = tb // c

    @pl.when(pl.program_id(1) == 0)
    def _():
        prev_ref[...] = jnp.zeros(prev_ref.shape, F32)
        st_ref[...] = jnp.zeros(st_ref.shape, F32)

    cw = cw_ref[...]
    for idx, (src, dst) in enumerate(((zq_ref, q_s), (zk_ref, k_s), (zv_ref, v_s))):
        x = src[...]
        y = _silu(_causal_conv(x, prev_ref[idx], cw[:, idx * GROUP:(idx + 1) * GROUP]))
        prev_ref[idx] = x[tb - SUBLANE:tb]
        for h in range(N_HEADS):
            yh = y[:, h * HEAD:(h + 1) * HEAD]
            if idx == 0:
                yh = _l2norm(yh) * HEAD ** -0.5
            elif idx == 1:
                yh = _l2norm(yh)
            for ci in range(nc):
                dst[h, ci] = yh[ci * c:(ci + 1) * c]

    zs = zs_ref[...]
    lane = lax.broadcasted_iota(jnp.int32, (1, SMALL), 1)
    g = -jnp.exp(alog_ref[...]) * _softplus(zs + dt_ref[...])
    gcum = _seg_cumsum(jnp.where(lane < N_HEADS, g, 0.0), c)
    gball = jnp.where(lane < N_HEADS, gcum, _sigmoid(zs))
    for ci in range(nc):
        gb_s[ci] = gball[ci * c:(ci + 1) * c]

    ii = lax.broadcasted_iota(jnp.int32, (1, c, c), 1)
    jj = lax.broadcasted_iota(jnp.int32, (1, c, c), 2)
    onehot = jnp.where(lax.broadcasted_iota(jnp.int32, (nc, SUBLANE, SMALL), 1)
                       == lax.broadcasted_iota(jnp.int32, (nc, SUBLANE, SMALL), 2), 1.0, 0.0)
    gb = gb_s[...]
    grow = lax.dot_general(onehot, gb, _BNT, precision=HI, preferred_element_type=F32)
    for h in range(N_HEADS):
        q, k, v = q_s[h], k_s[h], v_s[h]
        gcol = gb[:, :, h:h + 1]
        beta = gb[:, :, N_HEADS + h:N_HEADS + h + 1]
        dec = jnp.exp(jnp.where(ii >= jj, gcol - grow[:, h:h + 1, :], -jnp.inf))
        kb = k * beta
        kbf = k.astype(BF16)
        lm = jnp.where(ii > jj, _bdot(kb.astype(BF16), kbf, _BNT) * dec, 0.0)
        tinv = _unit_lower_inverse(lm, ii, jj)
        egc = jnp.exp(gcol)
        uw = _bmm3(_split_bf16(tinv), _split_bf16(jnp.concatenate([v * beta, kb * egc], axis=2)))
        u_s[h] = uw[:, :, :HEAD]
        w_s[h] = uw[:, :, HEAD:]
        a_s[h] = _bdot(q.astype(BF16), kbf, _BNT) * dec
        q_s[h] = q * egc
        k_s[h] = k * jnp.exp(gcol[:, c - 1:c, :] - gcol)

    ng = ng_ref[...]
    for ci in range(nc):
        r = slice(ci * c, (ci + 1) * c)
        for h in range(N_HEADS):
            sl = slice(h * HEAD, (h + 1) * HEAD)
            st = st_ref[h]
            v_new = u_s[h, ci] - _mm(w_s[h, ci], st)
            o = _mm(q_s[h, ci], st) + _mm(a_s[h, ci], v_new)
            st_ref[h] = jnp.exp(gb_s[ci, c - 1:c, h:h + 1]) * st + _mm(k_s[h, ci], v_new, _TN)
            o_ref[r, sl] = _head_rms_gate(o, ng, zg_ref[r, sl])


def _gdn_mixer(z, zs, conv_w, alog_row, dt_row, norm_g, layer, batch, seq):
    m = z.shape[0]
    tb = _pick(seq, (512, 256, 128, 64))
    per_b = seq // tb
    nc = tb // GDN_CHUNK
    nconv = conv_w.shape[-1]
    row = lambda a: pl.BlockSpec((None, 1, a.shape[-1]), lambda b, t: (layer, 0, 0))
    return pl.pallas_call(
        functools.partial(_gdn_kernel, tb=tb),
        grid=(batch, per_b),
        in_specs=[_zspec(tb, per_b, G_GDN_Q), _zspec(tb, per_b, G_GDN_K), _zspec(tb, per_b, G_GDN_V),
                  _zspec(tb, per_b, G_GDN_GATE),
                  pl.BlockSpec((tb, SMALL), lambda b, t: (b * per_b + t, 0)),
                  pl.BlockSpec((None, CONV_W, nconv), lambda b, t: (layer, 0, 0)),
                  row(alog_row), row(dt_row), row(norm_g)],
        out_specs=pl.BlockSpec((tb, GROUP), lambda b, t: (b * per_b + t, 0)),
        out_shape=jax.ShapeDtypeStruct((m, GROUP), BF16),
        scratch_shapes=[pltpu.VMEM((3, SUBLANE, GROUP), F32)]
                       + [pltpu.VMEM((N_HEADS, nc, GDN_CHUNK, HEAD), F32) for _ in range(3)]
                       + [pltpu.VMEM((nc, GDN_CHUNK, SMALL), F32)]
                       + [pltpu.VMEM((N_HEADS, nc, GDN_CHUNK, HEAD), F32) for _ in range(2)]
                       + [pltpu.VMEM((N_HEADS, nc, GDN_CHUNK, GDN_CHUNK), F32),
                          pltpu.VMEM((N_HEADS, HEAD, HEAD), F32)],
        compiler_params=_params(("parallel", "arbitrary"), 8 * tb * GROUP * 4, 2 * tb * GROUP * 2,
                                6 * tb * GROUP * 4, 8 * tb * GROUP * 4),
        name="gated_deltanet",
    )(z, z, z, z, zs, conv_w, alog_row, dt_row, norm_g)


def _gla_head(q, k, v, logf, st_ref, h, zg_ref, ng, o_ref):
    tb = q.shape[0]
    c = GLA_CHUNK
    nc = tb // c
    row = lax.broadcasted_iota(jnp.int32, (tb, HEAD), 0)
    ii = lax.broadcasted_iota(jnp.int32, (1, c, c), 1)
    jj = lax.broadcasted_iota(jnp.int32, (1, c, c), 2)
    chunks = lambda a: a.reshape(nc, c, HEAD).astype(BF16)

    b = _seg_cumsum(logf, c)
    e = b
    scores = jnp.where(ii == jj, _bdot(chunks(q), chunks(k), _BNT), 0.0)
    hs, bit = 1, 0
    while hs < c:
        qh = q * jnp.exp(jnp.minimum(b - pltpu.roll(e, hs, axis=0), 0.0))
        kh = k * jnp.exp(e - b)
        sel = ((ii >> (bit + 1)) == (jj >> (bit + 1))) & ((ii & hs) != 0) & ((jj & hs) == 0)
        scores = jnp.where(sel, _bdot(chunks(qh), chunks(kh), _BNT), scores)
        e = jnp.where((row & hs) != 0, e, pltpu.roll(e, tb - hs, axis=0))
        hs, bit = hs * 2, bit + 1
    qd = chunks(q * jnp.exp(b))
    kd = chunks(k * jnp.exp(e - b))
    vb = chunks(v)
    intra = _bdot(scores.astype(BF16), vb, _BNN)
    sl = slice(h * HEAD, (h + 1) * HEAD)
    for ci in range(nc):
        r = slice(ci * c, (ci + 1) * c)
        st = st_ref[h]
        o = intra[ci] + lax.dot_general(qd[ci], st.astype(BF16), _NT, preferred_element_type=F32)
        st_ref[h] = st * jnp.exp(e[ci * c:ci * c + 1, :]) + lax.dot_general(vb[ci], kd[ci], _TN, preferred_element_type=F32)
        o_ref[r, sl] = _head_rms_gate(o, ng, zg_ref[r, sl])


def _hgrn_kernel(zq_ref, zf_ref, zi_ref, zg_ref, lb_ref, ng_ref, o_ref, st_ref, *, layer):
    @pl.when(pl.program_id(1) == 0)
    def _():
        st_ref[...] = jnp.zeros(st_ref.shape, F32)

    x = lb_ref[...]
    e = jnp.exp(x - jnp.max(x, axis=0, keepdims=True))
    sm = e / jnp.sum(e, axis=0, keepdims=True)
    lb_all = jnp.zeros((1, GROUP), F32)
    for j in range(1, layer + 1):
        lb_all = lb_all + sm[j:j + 1, :]
    ng = ng_ref[...]
    for h in range(N_HEADS):
        sl = slice(h * HEAD, (h + 1) * HEAD)
        lb = lb_all[:, sl]
        f = lb + (1.0 - lb) * _sigmoid(zf_ref[:, sl])
        q = _silu(zq_ref[:, sl]) * HEAD ** -0.5
        _gla_head(q, 1.0 - f, zi_ref[:, sl], jnp.log(f), st_ref, h, zg_ref, ng, o_ref)


def _gla_kernel(zq_ref, zk_ref, zv_ref, zg_ref, zs_ref, w2_ref, gkb_ref, ng_ref, o_ref, st_ref):
    @pl.when(pl.program_id(1) == 0)
    def _():
        st_ref[...] = jnp.zeros(st_ref.shape, F32)

    zs = zs_ref[...]
    lane = lax.broadcasted_iota(jnp.int32, (1, HEAD), 1)
    ng = ng_ref[...]
    for h in range(N_HEADS):
        sl = slice(h * HEAD, (h + 1) * HEAD)
        gk = _log_sigmoid(_mmh(zs, w2_ref[h]) + gkb_ref[h]) / GLA_GATE_NORM
        logf = jnp.where(lane < GLA_DK, gk, 0.0)
        _gla_head(zq_ref[:, sl] * GLA_DK ** -0.5, zk_ref[:, sl], zv_ref[:, sl], logf, st_ref, h, zg_ref, ng, o_ref)


def _gla_scratch(tb):
    return [pltpu.VMEM((N_HEADS, HEAD, HEAD), F32)]


def _hgrn_mixer(z, lower_bounds, norm_g, layer, batch, seq):
    m = z.shape[0]
    tb = _pick(seq, (512, 256, 128, 64))
    per_b = seq // tb
    depth = lower_bounds.shape[0]
    return pl.pallas_call(
        functools.partial(_hgrn_kernel, layer=layer),
        grid=(batch, per_b),
        in_specs=[_zspec(tb, per_b, G_HGRN_Q), _zspec(tb, per_b, G_HGRN_F), _zspec(tb, per_b, G_HGRN_I),
                  _zspec(tb, per_b, G_HGRN_GATE),
                  pl.BlockSpec((depth, GROUP), lambda b, t: (0, 0)),
                  pl.BlockSpec((None, 1, HEAD), lambda b, t: (layer, 0, 0))],
        out_specs=pl.BlockSpec((tb, GROUP), lambda b, t: (b * per_b + t, 0)),
        out_shape=jax.ShapeDtypeStruct((m, GROUP), BF16),
        scratch_shapes=_gla_scratch(tb),
        compiler_params=_params(("parallel", "arbitrary"), 8 * tb * GROUP * 4, 2 * tb * GROUP * 2,
                                4 * tb * GROUP * 4, 8 * tb * GROUP * 4),
        name="hgrn2",
    )(z, z, z, z, lower_bounds, norm_g)


def _gla_mixer(z, zs, w2_pad, gkb_pad, norm_g, layer, batch, seq):
    m = z.shape[0]
    tb = _pick(seq, (512, 256, 128, 64))
    per_b = seq // tb
    return pl.pallas_call(
        _gla_kernel,
        grid=(batch, per_b),
        in_specs=[_zspec(tb, per_b, G_GLA_Q), _zspec(tb, per_b, G_GLA_K), _zspec(tb, per_b, G_GLA_V),
                  _zspec(tb, per_b, G_GLA_GATE),
                  pl.BlockSpec((tb, SMALL), lambda b, t: (b * per_b + t, 0)),
                  pl.BlockSpec((None, N_HEADS, SMALL, HEAD), lambda b, t: (layer, 0, 0, 0)),
                  pl.BlockSpec((None, N_HEADS, 1, HEAD), lambda b, t: (layer, 0, 0, 0)),
                  pl.BlockSpec((None, 1, HEAD), lambda b, t: (layer, 0, 0))],
        out_specs=pl.BlockSpec((tb, GROUP), lambda b, t: (b * per_b + t, 0)),
        out_shape=jax.ShapeDtypeStruct((m, GROUP), BF16),
        scratch_shapes=_gla_scratch(tb),
        compiler_params=_params(("parallel", "arbitrary"), 8 * tb * GROUP * 4, 2 * tb * GROUP * 2,
                                4 * tb * GROUP * 4, 8 * tb * GROUP * 4),
        name="gla",
    )(z, z, z, z, zs, w2_pad, gkb_pad, norm_g)


def _rope_kernel(pos_ref, inv_ref, cos_ref, sin_ref):
    ang = pos_ref[...].astype(F32) * inv_ref[...]
    lane = lax.broadcasted_iota(jnp.int32, ang.shape, 1)
    cos_ref[...] = jnp.cos(ang)
    sin_ref[...] = jnp.where(lane < HEAD // 2, -jnp.sin(ang), jnp.sin(ang))


def _rope_tables(positions):
    m = positions.size
    half = HEAD // 2
    inv = (ROPE_BASE ** (-np.arange(half, dtype=np.float64) / half)).astype(np.float32)
    inv2 = jnp.asarray(np.concatenate([inv, inv])[None, :])
    tb = _pick(m, (1024, 512, 256, 128, 64))
    return pl.pallas_call(
        _rope_kernel,
        grid=(m // tb,),
        in_specs=[pl.BlockSpec((tb, 1), lambda i: (i, 0)), pl.BlockSpec((1, HEAD), lambda i: (0, 0))],
        out_specs=[pl.BlockSpec((tb, HEAD), lambda i: (i, 0)), pl.BlockSpec((tb, HEAD), lambda i: (i, 0))],
        out_shape=[jax.ShapeDtypeStruct((m, HEAD), F32), jax.ShapeDtypeStruct((m, HEAD), F32)],
        compiler_params=_params(("parallel",), 2 * tb * LANE * 4, 4 * tb * HEAD * 4, 4 * tb * HEAD * 4),
        name="rope_tables",
    )(positions.reshape(m, 1), inv2)


def _ret_kernel(zq_ref, zk_ref, zv_ref, zg_ref, cos_ref, sin_ref, ng_ref, nb_ref, o_ref, st_ref, *, tb):
    c = RET_CHUNK

    @pl.when(pl.program_id(1) == 0)
    def _():
        st_ref[...] = jnp.zeros(st_ref.shape, F32)

    ii = lax.broadcasted_iota(jnp.int32, (c, c), 0)
    jj = lax.broadcasted_iota(jnp.int32, (c, c), 1)
    pos = lax.broadcasted_iota(jnp.int32, (c, 1), 0).astype(F32)
    ng, nb = ng_ref[...], nb_ref[...]
    for h in range(N_HEADS):
        sl = slice(h * HEAD, (h + 1) * HEAD)
        lg = LOG_GAMMA[h]
        intra = jnp.where(ii >= jj, jnp.exp((ii - jj).astype(F32) * lg), 0.0)
        xi = jnp.exp((pos + 1.0) * lg)
        zeta = jnp.exp((c - 1.0 - pos) * lg)
        for ci in range(tb // c):
            r = slice(ci * c, (ci + 1) * c)
            cos, sin = cos_ref[r, :], sin_ref[r, :]
            xq, xk = zq_ref[r, sl], zk_ref[r, sl]
            q = xq * cos + pltpu.roll(xq, HEAD // 2, axis=1) * sin
            k = (xk * cos + pltpu.roll(xk, HEAD // 2, axis=1) * sin) * HEAD ** -0.5
            v = zv_ref[r, sl]
            st = st_ref[h]
            o = _mm(_mm(q, k, _NT) * intra, v) + _mm(q * xi, st)
            st_ref[h] = math.exp(c * lg) * st + _mm(k * zeta, v, _TN)
            o_ref[r, sl] = _head_group_gate(o, ng, nb, zg_ref[r, sl])


def _ret_mixer(z, cos2, sin2, norm_g, norm_b, layer, batch, seq):
    m = z.shape[0]
    tb = _pick(seq, (512, 256))
    per_b = seq // tb
    rowspec = pl.BlockSpec((tb, HEAD), lambda b, t: (b * per_b + t, 0))
    prm = pl.BlockSpec((None, 1, HEAD), lambda b, t: (layer, 0, 0))
    return pl.pallas_call(
        functools.partial(_ret_kernel, tb=tb),
        grid=(batch, per_b),
        in_specs=[_zspec(tb, per_b, G_RET_Q), _zspec(tb, per_b, G_RET_K), _zspec(tb, per_b, G_RET_V),
                  _zspec(tb, per_b, G_RET_GATE), rowspec, rowspec, prm, prm],
        out_specs=pl.BlockSpec((tb, GROUP), lambda b, t: (b * per_b + t, 0)),
        out_shape=jax.ShapeDtypeStruct((m, GROUP), BF16),
        scratch_shapes=[pltpu.VMEM((N_HEADS, HEAD, HEAD), F32)],
        compiler_params=_params(("parallel", "arbitrary"), 8 * tb * GROUP * 4, 2 * tb * GROUP * 2,
                                4 * tb * HEAD * 4, 16 * RET_CHUNK * RET_CHUNK * 4),
        name="retention",
    )(z, z, z, z, cos2, sin2, norm_g, norm_b)


def _merge_kernel(u_ref, y0, y1, y2, y3, g0, g1, g2, g3, b0, b1, b2, b3, wb_ref, o_ref):
    u = u_ref[...]
    acc = None
    for m, (y, wg, bg) in enumerate(((y0, g0, b0), (y1, g1, b1), (y2, g2, b2), (y3, g3, b3))):
        gate = _sigmoid(lax.dot_general(u, wg[...], _NN, preferred_element_type=F32) + bg[...])
        term = gate * lax.dot_general(y[...], wb_ref[m], _NN, preferred_element_type=F32)
        acc = term if acc is None else acc + term
    o_ref[...] = acc.astype(BF16)


def _merge(u, ys, w_gate, b_gate, w_branch, layer):
    m, d = u.shape
    tm = _pick(m, (1024, 512, 256, 128))
    td = _pick(d, (512, 256, 128))
    nd = d // td
    yspec = pl.BlockSpec((tm, GROUP), lambda i, j: (i, 0))
    gspecs = [pl.BlockSpec((None, d, td), functools.partial(lambda i, j, br: (layer, 0, br * nd + j), br=br))
              for br in range(N_BRANCH)]
    bspecs = [pl.BlockSpec((None, 1, td), functools.partial(lambda i, j, br: (layer, 0, br * nd + j), br=br))
              for br in range(N_BRANCH)]
    return pl.pallas_call(
        _merge_kernel,
        grid=(m // tm, nd),
        in_specs=[pl.BlockSpec((tm, d), lambda i, j: (i, 0))] + [yspec] * N_BRANCH + gspecs + bspecs
                 + [pl.BlockSpec((None, N_BRANCH, GROUP, td), lambda i, j: (layer, 0, 0, j))],
        out_specs=pl.BlockSpec((tm, td), lambda i, j: (i, j)),
        out_shape=jax.ShapeDtypeStruct((m, d), BF16),
        compiler_params=_params(("parallel", "arbitrary"), 2 * tm * d * 2, 2 * N_BRANCH * tm * GROUP * 2,
                                2 * N_BRANCH * d * td * 2, 2 * N_BRANCH * GROUP * td * 2, 2 * tm * td * 2,
                                6 * tm * td * 4),
        name="branch_merge",
    )(u, *ys, *([w_gate] * N_BRANCH), *([b_gate] * N_BRANCH), w_branch)


def _outproj_kernel(x_ref, w_ref, h_ref, gate_ref, o_ref):
    o_ref[...] = h_ref[...] + gate_ref[...] * lax.dot_general(x_ref[...], w_ref[...], _NN, preferred_element_type=F32)


def _out_projection(x, w_out, h, gate, layer, seq):
    m, d = h.shape
    tm = _pick(seq, (512, 256, 128))
    per_b = seq // tm
    return pl.pallas_call(
        _outproj_kernel,
        grid=(m // tm,),
        in_specs=[pl.BlockSpec((tm, d), lambda i: (i, 0)),
                  pl.BlockSpec((None, d, d), lambda i: (layer, 0, 0)),
                  pl.BlockSpec((tm, d), lambda i: (i, 0)),
                  pl.BlockSpec((None, 1, d), lambda i: (i // per_b, 0, 0))],
        out_specs=pl.BlockSpec((tm, d), lambda i: (i, 0)),
        out_shape=jax.ShapeDtypeStruct((m, d), F32),
        compiler_params=_params(("parallel",), 2 * tm * d * 2, 2 * d * d * 2, 4 * tm * d * 4, 2 * tm * d * 4),
        name="out_projection",
    )(x, w_out, h, gate)


def _ffn_kernel(h_ref, g_ref, sc_ref, sh_ref, gt_ref, wg_ref, wu_ref, wd_ref, o_ref, u_s, acc_s):
    f = pl.program_id(1)

    @pl.when(f == 0)
    def _():
        u_s[...] = (_rms(h_ref[...], g_ref[...]) * (1.0 + sc_ref[...]) + sh_ref[...]).astype(BF16)
        acc_s[...] = jnp.zeros(acc_s.shape, F32)

    u = u_s[...]
    a = lax.dot_general(u, wg_ref[...], _NN, preferred_element_type=F32)
    b = lax.dot_general(u, wu_ref[...], _NN, preferred_element_type=F32)
    acc_s[...] += lax.dot_general((_silu(a) * b).astype(BF16), wd_ref[...], _NN, preferred_element_type=F32)

    @pl.when(f == pl.num_programs(1) - 1)
    def _():
        o_ref[...] = h_ref[...] + gt_ref[...] * acc_s[...]


def _ffn(h, gain, scale, shift, gate, w_gate, w_up, w_down, layer, seq):
    m, d = h.shape
    dff = w_gate.shape[-1]
    tm = _pick(seq, (512, 256, 128))
    tf = _pick(dff, (512, 256, 128))
    per_b = seq // tm
    mod = pl.BlockSpec((None, 1, d), lambda i, f: (i // per_b, 0, 0))
    return pl.pallas_call(
        _ffn_kernel,
        grid=(m // tm, dff // tf),
        in_specs=[pl.BlockSpec((tm, d), lambda i, f: (i, 0)),
                  pl.BlockSpec((None, 1, d), lambda i, f: (layer, 0, 0)),
                  mod, mod, mod,
                  pl.BlockSpec((None, d, tf), lambda i, f: (layer, 0, f)),
                  pl.BlockSpec((None, d, tf), lambda i, f: (layer, 0, f)),
                  pl.BlockSpec((None, tf, d), lambda i, f: (layer, f, 0))],
        out_specs=pl.BlockSpec((tm, d), lambda i, f: (i, 0)),
        out_shape=jax.ShapeDtypeStruct((m, d), F32),
        scratch_shapes=[pltpu.VMEM((tm, d), BF16), pltpu.VMEM((tm, d), F32)],
        compiler_params=_params(("parallel", "arbitrary"), 4 * tm * d * 4, 6 * d * tf * 2, tm * d * 6,
                                4 * tm * tf * 4, tm * d * 4),
        name="swiglu_ffn",
    )(h, gain, scale, shift, gate, w_gate, w_up, w_down)


def _final_norm_kernel(x_ref, g_ref, o_ref):
    o_ref[...] = _rms(x_ref[...], g_ref[...])


def _final_norm(h, gain):
    m, d = h.shape
    tm = _pick(m, (512, 256, 128))
    return pl.pallas_call(
        _final_norm_kernel,
        grid=(m // tm,),
        in_specs=[pl.BlockSpec((tm, d), lambda i: (i, 0)), pl.BlockSpec((1, d), lambda i: (0, 0))],
        out_specs=pl.BlockSpec((tm, d), lambda i: (i, 0)),
        out_shape=jax.ShapeDtypeStruct((m, d), F32),
        compiler_params=_params(("parallel",), 4 * tm * d * 4, 2 * tm * d * 4),
        name="final_norm",
    )(h, gain.reshape(1, d))


def _split_offsets():
    widths = (("gdn_q", GROUP), ("gdn_k", GROUP), ("gdn_v", GROUP), ("gdn_a", N_HEADS), ("gdn_b", N_HEADS),
              ("gdn_gate", GROUP), ("hgrn_q", GROUP), ("hgrn_f", GROUP), ("hgrn_i", GROUP), ("hgrn_gate", GROUP),
              ("ret_q", GROUP), ("ret_k", GROUP), ("ret_v", GROUP), ("ret_gate", GROUP),
              ("gla_q", N_HEADS * GLA_DK), ("gla_k", N_HEADS * GLA_DK), ("gla_v", GROUP),
              ("gla_gk_lr", GLA_LOWRANK), ("gla_gate", GROUP))
    out, off = {}, 0
    for name, w in widths:
        out[name] = (off, w)
        off += w
    return out, off


def _permute_w_in(w_in):
    offs, total = _split_offsets()
    assert w_in.shape[-1] == total
    depth, d, _ = w_in.shape

    def cols(name):
        o, w = offs[name]
        return w_in[:, :, o:o + w]

    def pad_heads(name):
        x = cols(name).reshape(depth, d, N_HEADS, GLA_DK)
        return jnp.pad(x, ((0, 0), (0, 0), (0, 0), (0, HEAD - GLA_DK))).reshape(depth, d, GROUP)

    big = [cols("gdn_q"), cols("gdn_k"), cols("gdn_v"), cols("gdn_gate"),
           cols("hgrn_q"), cols("hgrn_f"), cols("hgrn_i"), cols("hgrn_gate"),
           cols("ret_q"), cols("ret_k"), cols("ret_v"), cols("ret_gate"),
           pad_heads("gla_q"), pad_heads("gla_k"), cols("gla_v"), cols("gla_gate")]
    small = jnp.concatenate([cols("gdn_a"), cols("gdn_b"), cols("gla_gk_lr")], axis=-1)
    small = jnp.pad(small, ((0, 0), (0, 0), (0, SMALL - small.shape[-1])))
    return jnp.concatenate(big, axis=-1).astype(BF16), small.astype(BF16)


def _lane_row(x, width=SMALL):
    return jnp.pad(x.astype(F32), ((0, 0), (0, width - x.shape[-1])))[:, None, :]


def kernel(x, c, positions, ada_w, ada_b, norm_mix_g, w_in, gdn_conv_w, gdn_a_log, gdn_dt_bias, gdn_norm_g, hgrn_lower_bounds, hgrn_norm_g, ret_norm_g, ret_norm_b, gla_gk_w2, gla_gk_b, gla_norm_g, w_branch, w_merge_gate, b_merge_gate, w_out, norm_ffn_g, ffn_w_gate, ffn_w_up, ffn_w_down, final_norm_g):
    batch, seq, d = x.shape
    depth = w_in.shape[0]
    m = batch * seq
    assert seq % max(RET_CHUNK, GDN_CHUNK) == 0 and d % LANE == 0

    w_big, w_small = _permute_w_in(w_in)
    w_mg = w_merge_gate.astype(BF16)
    b_mg = b_merge_gate.reshape(depth, 1, N_BRANCH * d)
    w_br = w_branch.astype(BF16)
    w_o = w_out.astype(BF16)
    w_fg, w_fu, w_fd = ffn_w_gate.astype(BF16), ffn_w_up.astype(BF16), ffn_w_down.astype(BF16)
    alog_row, dt_row = _lane_row(gdn_a_log), _lane_row(gdn_dt_bias)
    w2 = gla_gk_w2.reshape(depth, GLA_LOWRANK, N_HEADS, GLA_DK).transpose(0, 2, 1, 3)
    w2_pad = jnp.zeros((depth, N_HEADS, SMALL, HEAD), F32).at[:, :, 2 * N_HEADS:2 * N_HEADS + GLA_LOWRANK, :GLA_DK].set(w2)
    gkb_pad = jnp.pad(gla_gk_b.reshape(depth, N_HEADS, 1, GLA_DK), ((0, 0), (0, 0), (0, 0), (0, HEAD - GLA_DK)))
    row3 = lambda a: a.reshape(depth, 1, a.shape[-1])

    mod = _modulation(c, ada_w, ada_b)
    cos2, sin2 = _rope_tables(positions)

    h = x.reshape(m, d)
    for l in range(depth):
        shift_m, scale_m, gate_m, shift_f, scale_f, gate_f = [mod[l, :, i * d:(i + 1) * d][:, None, :] for i in range(6)]
        u, z, zs = _in_projection(h, row3(norm_mix_g), scale_m, shift_m, w_big, w_small, l, seq)
        ys = (_gdn_mixer(z, zs, gdn_conv_w, alog_row, dt_row, row3(gdn_norm_g), l, batch, seq),
              _hgrn_mixer(z, hgrn_lower_bounds, row3(hgrn_norm_g), l, batch, seq),
              _ret_mixer(z, cos2, sin2, row3(ret_norm_g), row3(ret_norm_b), l, batch, seq),
              _gla_mixer(z, zs, w2_pad, gkb_pad, row3(gla_norm_g), l, batch, seq))
        merged = _merge(u, ys, w_mg, b_mg, w_br, l)
        h = _out_projection(merged, w_o, h, gate_m, l, seq)
        h = _ffn(h, row3(norm_ffn_g), scale_f, shift_f, gate_f, w_fg, w_fu, w_fd, l, seq)
    return _final_norm(h, final_norm_g).reshape(batch, seq, d)
```

```python
import functools
import math

import numpy as np
import jax
import jax.numpy as jnp
from jax import lax
from jax.experimental import pallas as pl
from jax.experimental.pallas import tpu as pltpu

F32 = jnp.float32
BF16 = jnp.bfloat16
HI = lax.Precision.HIGHEST

N_HEADS = 4
HEAD = 128
GROUP = N_HEADS * HEAD
GLA_DK = 64
GLA_LOWRANK = 16
GLA_GATE_NORM = 16.0
CONV_W = 4
EPS = 1e-6
ROPE_BASE = 10000.0
N_BRANCH = 4

GDN_CHUNK = 64
GLA_CHUNK = 128
RET_CHUNK = 256
SMALL = 128

LANE = 128
SUBLANE = 8
VMEM_CAP_BYTES = 60000 * 1024
INTERNAL_SCRATCH_BYTES = 12 * 1024 * 1024

G_GDN_Q, G_GDN_K, G_GDN_V, G_GDN_GATE = 0, 1, 2, 3
G_HGRN_Q, G_HGRN_F, G_HGRN_I, G_HGRN_GATE = 4, 5, 6, 7
G_RET_Q, G_RET_K, G_RET_V, G_RET_GATE = 8, 9, 10, 11
G_GLA_Q, G_GLA_K, G_GLA_V, G_GLA_GATE = 12, 13, 14, 15
N_GROUPS = 16

LOG2_E = 1.4426950408889634
LOG_GAMMA = tuple(float(v) for v in np.log1p(-np.exp(np.linspace(math.log(1.0 / 32.0), math.log(1.0 / 512.0), N_HEADS))))

_NN = (((1,), (0,)), ((), ()))
_NT = (((1,), (1,)), ((), ()))
_TN = (((0,), (0,)), ((), ()))


def _mm(a, b, dims=_NN):
    return lax.dot_general(a.astype(BF16), b.astype(BF16), dims, preferred_element_type=F32)


def _mmh(a, b, dims=_NN):
    return lax.dot_general(a, b, dims, precision=HI, preferred_element_type=F32)


def _sigmoid(x):
    return 1.0 / (1.0 + jnp.exp(-x))


def _silu(x):
    return x * _sigmoid(x)


def _softplus(x):
    return jnp.maximum(x, 0.0) + jnp.log1p(jnp.exp(-jnp.abs(x)))


def _log_sigmoid(x):
    return -_softplus(-x)


def _rms(x, g):
    return x * lax.rsqrt(jnp.mean(x * x, axis=-1, keepdims=True) + EPS) * g


def _seg_cumsum(x, seg):
    row = lax.broadcasted_iota(jnp.int32, x.shape, 0) & (seg - 1)
    s = 1
    while s < seg:
        x = x + jnp.where(row >= s, pltpu.roll(x, s, axis=0), 0.0)
        s *= 2
    return x


def _pick(n, prefs):
    for p in prefs:
        if n % p == 0:
            return p
    return n


def _params(sem, *nbytes):
    need = int(sum(nbytes)) + INTERNAL_SCRATCH_BYTES
    return pltpu.CompilerParams(dimension_semantics=sem, vmem_limit_bytes=min(VMEM_CAP_BYTES, need))


def _mod_kernel(c_ref, w_ref, b_ref, o_ref):
    o_ref[...] = _mm(_silu(c_ref[...]), w_ref[...]) + b_ref[...]


def _modulation(c, ada_w, ada_b):
    depth, d, n = ada_w.shape
    b = c.shape[0]
    rows = -(-b // SUBLANE) * SUBLANE
    c_pad = jnp.zeros((rows, d), F32).at[:b].set(c)
    tn = _pick(n, (1536, 1024, 512, 256, 128))
    out = pl.pallas_call(
        _mod_kernel,
        grid=(depth, n // tn),
        in_specs=[pl.BlockSpec((rows, d), lambda l, j: (0, 0)),
                  pl.BlockSpec((None, d, tn), lambda l, j: (l, 0, j)),
                  pl.BlockSpec((None, 1, tn), lambda l, j: (l, 0, j))],
        out_specs=pl.BlockSpec((None, rows, tn), lambda l, j: (l, 0, j)),
        out_shape=jax.ShapeDtypeStruct((depth, rows, n), F32),
        compiler_params=_params(("parallel", "parallel"), 2 * d * tn * 4, 4 * rows * tn * 4, 8 * rows * d * 4),
        name="adaln_modulation",
    )(c_pad, ada_w, ada_b.reshape(depth, 1, n))
    return out[:, :b]


def _norm_mod(x, g, scale, shift):
    return (_rms(x, g) * (1.0 + scale) + shift).astype(BF16)


def _inproj_kernel(h_ref, g_ref, sc_ref, sh_ref, w_ref, ws_ref, u_ref, z_ref, zs_ref):
    @pl.when(pl.program_id(1) == 0)
    def _():
        u = _norm_mod(h_ref[...], g_ref[...], sc_ref[...], sh_ref[...])
        u_ref[...] = u
        zs_ref[...] = lax.dot_general(u, ws_ref[...], _NN, preferred_element_type=F32)

    z_ref[...] = lax.dot_general(u_ref[...], w_ref[...], _NN, preferred_element_type=F32).astype(z_ref.dtype)


def _in_projection(h, gain, scale, shift, w_big, w_small, layer, seq):
    m, d = h.shape
    nb = w_big.shape[-1]
    tm = _pick(seq, (1024, 512, 256, 128))
    tn = _pick(nb, (1024, 512, 256, 128))
    per_b = seq // tm
    return pl.pallas_call(
        _inproj_kernel,
        grid=(m // tm, nb // tn),
        in_specs=[pl.BlockSpec((tm, d), lambda i, j: (i, 0)),
                  pl.BlockSpec((None, 1, d), lambda i, j: (layer, 0, 0)),
                  pl.BlockSpec((None, 1, d), lambda i, j: (i // per_b, 0, 0)),
                  pl.BlockSpec((None, 1, d), lambda i, j: (i // per_b, 0, 0)),
                  pl.BlockSpec((None, d, tn), lambda i, j: (layer, 0, j)),
                  pl.BlockSpec((None, d, SMALL), lambda i, j: (layer, 0, 0))],
        out_specs=[pl.BlockSpec((tm, d), lambda i, j: (i, 0)),
                   pl.BlockSpec((tm, tn), lambda i, j: (i, j)),
                   pl.BlockSpec((tm, SMALL), lambda i, j: (i, 0))],
        out_shape=[jax.ShapeDtypeStruct((m, d), BF16),
                   jax.ShapeDtypeStruct((m, nb), BF16),
                   jax.ShapeDtypeStruct((m, SMALL), F32)],
        compiler_params=_params(("parallel", "arbitrary"), 2 * tm * d * 4, 2 * tm * d * 2, 2 * d * tn * 2,
                                2 * tm * tn * 2, 2 * d * SMALL * 2, 2 * tm * SMALL * 4, 2 * tm * d * 4),
        name="norm_in_projection",
    )(h, gain, scale, shift, w_big, w_small)


def _head_rms_gate(o, g, gate):
    return (o * lax.rsqrt(jnp.mean(o * o, axis=-1, keepdims=True) + EPS) * g * _silu(gate)).astype(BF16)


def _head_group_gate(o, g, b, gate):
    mu = jnp.mean(o, axis=-1, keepdims=True)
    oc = o - mu
    var = jnp.mean(oc * oc, axis=-1, keepdims=True)
    return ((oc * lax.rsqrt(var + EPS) * g + b) * _silu(gate)).astype(BF16)


def _f32(ref, rows, cols):
    return ref[rows, cols].astype(F32)


def _zspec(tb, per_b, group):
    return pl.BlockSpec((tb, GROUP), lambda b, t: (b * per_b + t, group))


def _causal_conv(x, prev, w):
    acc = x * w[CONV_W - 1:CONV_W]
    rows = lax.broadcasted_iota(jnp.int32, prev.shape, 0)
    for s in range(1, CONV_W):
        xs = pltpu.roll(x, s, axis=0)
        head = jnp.where(rows < s, pltpu.roll(prev, s, axis=0), xs[0:SUBLANE])
        xs = jnp.concatenate([head, xs[SUBLANE:]], axis=0)
        acc = acc + xs * w[CONV_W - 1 - s:CONV_W - s]
    return acc


def _l2norm(x):
    return x * lax.rsqrt(jnp.sum(x * x, axis=-1, keepdims=True) + EPS)


def _split_bf16(a):
    hi = a.astype(BF16)
    return hi, (a - hi.astype(F32)).astype(BF16)


_BNN = (((2,), (1,)), ((0,), (0,)))
_BNT = (((2,), (2,)), ((0,), (0,)))


def _bdot(a, b, dims):
    return lax.dot_general(a, b, dims, preferred_element_type=F32)


def _bmm3(a, b, dims=_BNN):
    (ah, al), (bh, bl) = a, b
    return _bdot(ah, bh, dims) + _bdot(ah, bl, dims) + _bdot(al, bh, dims)


def _unit_lower_inverse(lm, ii, jj):
    c = lm.shape[-1]
    x = jnp.where(ii == jj, 1.0, 0.0) - jnp.where(((ii & 1) == 1) & (jj == ii - 1), lm, 0.0)
    lms = _split_bf16(lm)
    k = 1
    while (1 << k) < c:
        same = (ii >> (k + 1)) == (jj >> (k + 1))
        sel = same & (((ii >> k) & 1) == 1) & (((jj >> k) & 1) == 0)
        xs = _split_bf16(x)
        x = x - jnp.where(sel, _bmm3(_split_bf16(_bmm3(xs, lms)), xs), 0.0)
        k += 1
    return x


def _gdn_kernel(zq_ref, zk_ref, zv_ref, zg_ref, zs_ref, cw_ref, alog_ref, dt_ref, ng_ref, o_ref,
                prev_ref, q_s, k_s, v_s, gb_s, u_s, w_s, a_s, st_ref, *, tb):
    c = GDN_CHUNK
    nc = tb // c

    @pl.when(pl.program_id(1) == 0)
    def _():
        prev_ref[...] = jnp.zeros(prev_ref.shape, F32)
        st_ref[...] = jnp.zeros(st_ref.shape, F32)

    cw = cw_ref[...]
    for idx, (src, dst) in enumerate(((zq_ref, q_s), (zk_ref, k_s), (zv_ref, v_s))):
        x = src[...].astype(F32)
        y = _silu(_causal_conv(x, prev_ref[idx], cw[:, idx * GROUP:(idx + 1) * GROUP]))
        prev_ref[idx] = x[tb - SUBLANE:tb]
        for h in range(N_HEADS):
            yh = y[:, h * HEAD:(h + 1) * HEAD]
            if idx == 0:
                yh = _l2norm(yh) * HEAD ** -0.5
            elif idx == 1:
                yh = _l2norm(yh)
            for ci in range(nc):
                dst[h, ci] = yh[ci * c:(ci + 1) * c]

    zs = zs_ref[...]
    lane = lax.broadcasted_iota(jnp.int32, (1, SMALL), 1)
    g = -jnp.exp(alog_ref[...]) * _softplus(zs + dt_ref[...])
    gcum = _seg_cumsum(jnp.where(lane < N_HEADS, g, 0.0), c)
    gball = jnp.where(lane < N_HEADS, gcum, _sigmoid(zs))
    for ci in range(nc):
        gb_s[ci] = gball[ci * c:(ci + 1) * c]

    ii = lax.broadcasted_iota(jnp.int32, (1, c, c), 1)
    jj = lax.broadcasted_iota(jnp.int32, (1, c, c), 2)
    onehot = jnp.where(lax.broadcasted_iota(jnp.int32, (nc, SUBLANE, SMALL), 1)
                       == lax.broadcasted_iota(jnp.int32, (nc, SUBLANE, SMALL), 2), 1.0, 0.0)
    gb = gb_s[...]
    grow = lax.dot_general(onehot, gb, _BNT, precision=HI, preferred_element_type=F32)
    for h in range(N_HEADS):
        q, k, v = q_s[h], k_s[h], v_s[h]
        gcol = gb[:, :, h:h + 1]
        beta = gb[:, :, N_HEADS + h:N_HEADS + h + 1]
        dec = jnp.exp(jnp.where(ii >= jj, gcol - grow[:, h:h + 1, :], -jnp.inf))
        kb = k * beta
        kbf = k.astype(BF16)
        lm = jnp.where(ii > jj, _bdot(kb.astype(BF16), kbf, _BNT) * dec, 0.0)
        tinv = _unit_lower_inverse(lm, ii, jj)
        egc = jnp.exp(gcol)
        uw = _bmm3(_split_bf16(tinv), _split_bf16(jnp.concatenate([v * beta, kb * egc], axis=2)))
        u_s[h] = uw[:, :, :HEAD]
        w_s[h] = uw[:, :, HEAD:]
        a_s[h] = _bdot(q.astype(BF16), kbf, _BNT) * dec
        q_s[h] = q * egc
        k_s[h] = k * jnp.exp(gcol[:, c - 1:c, :] - gcol)

    ng = ng_ref[...]
    for ci in range(nc):
        r = slice(ci * c, (ci + 1) * c)
        for h in range(N_HEADS):
            sl = slice(h * HEAD, (h + 1) * HEAD)
            st = st_ref[h]
            v_new = u_s[h, ci] - _mm(w_s[h, ci], st)
            o = _mm(q_s[h, ci], st) + _mm(a_s[h, ci], v_new)
            st_ref[h] = jnp.exp(gb_s[ci, c - 1:c, h:h + 1]) * st + _mm(k_s[h, ci], v_new, _TN)
            o_ref[r, sl] = _head_rms_gate(o, ng, _f32(zg_ref, r, sl))


def _gdn_mixer(z, zs, conv_w, alog_row, dt_row, norm_g, layer, batch, seq):
    m = z.shape[0]
    tb = _pick(seq, (512, 256, 128, 64))
    per_b = seq // tb
    nc = tb // GDN_CHUNK
    nconv = conv_w.shape[-1]
    row = lambda a: pl.BlockSpec((None, 1, a.shape[-1]), lambda b, t: (layer, 0, 0))
    return pl.pallas_call(
        functools.partial(_gdn_kernel, tb=tb),
        grid=(batch, per_b),
        in_specs=[_zspec(tb, per_b, G_GDN_Q), _zspec(tb, per_b, G_GDN_K), _zspec(tb, per_b, G_GDN_V),
                  _zspec(tb, per_b, G_GDN_GATE),
                  pl.BlockSpec((tb, SMALL), lambda b, t: (b * per_b + t, 0)),
                  pl.BlockSpec((None, CONV_W, nconv), lambda b, t: (layer, 0, 0)),
                  row(alog_row), row(dt_row), row(norm_g)],
        out_specs=pl.BlockSpec((tb, GROUP), lambda b, t: (b * per_b + t, 0)),
        out_shape=jax.ShapeDtypeStruct((m, GROUP), BF16),
        scratch_shapes=[pltpu.VMEM((3, SUBLANE, GROUP), F32)]
                       + [pltpu.VMEM((N_HEADS, nc, GDN_CHUNK, HEAD), F32) for _ in range(3)]
                       + [pltpu.VMEM((nc, GDN_CHUNK, SMALL), F32)]
                       + [pltpu.VMEM((N_HEADS, nc, GDN_CHUNK, HEAD), F32) for _ in range(2)]
                       + [pltpu.VMEM((N_HEADS, nc, GDN_CHUNK, GDN_CHUNK), F32),
                          pltpu.VMEM((N_HEADS, HEAD, HEAD), F32)],
        compiler_params=_params(("parallel", "arbitrary"), 8 * tb * GROUP * 4, 2 * tb * GROUP * 2,
                                6 * tb * GROUP * 4, 8 * tb * GROUP * 4),
        name="gated_deltanet",
    )(z, z, z, z, zs, conv_w, alog_row, dt_row, norm_g)


def _gla_levels(tb):
    c = GLA_CHUNK
    row = lax.broadcasted_iota(jnp.int32, (tb, HEAD), 0)
    ii = lax.broadcasted_iota(jnp.int32, (1, c, c), 1)
    jj = lax.broadcasted_iota(jnp.int32, (1, c, c), 2)
    levels = []
    hs, bit = 1, 0
    while hs < c:
        pair = ((ii >> (bit + 1)) == (jj >> (bit + 1))) & ((ii & hs) != 0) & ((jj & hs) == 0)
        levels.append((hs, (row & hs) != 0, pair))
        hs, bit = hs * 2, bit + 1
    return ii == jj, levels


def _gla_head(q, k, v, logf, st_ref, h, zg_ref, ng, o_ref, diag, levels):
    tb = q.shape[0]
    c = GLA_CHUNK
    nc = tb // c
    chunks = lambda a: a.reshape(nc, c, HEAD).astype(BF16)

    b = _seg_cumsum(logf * LOG2_E, c)
    e = b
    scores = jnp.where(diag, _bdot(chunks(q), chunks(k), _BNT), 0.0)
    for hs, upper, pair in levels:
        expo = jnp.minimum(jnp.where(upper, b - pltpu.roll(e, hs, axis=0), e - b), 0.0)
        x = chunks(jnp.where(upper, q, k) * jnp.exp2(expo))
        scores = jnp.where(pair, _bdot(x, x, _BNT), scores)
        e = jnp.where(upper, e, pltpu.roll(e, tb - hs, axis=0))
    qd = chunks(q * jnp.exp2(b))
    kd = chunks(k * jnp.exp2(e - b))
    vb = chunks(v)
    intra = _bdot(scores.astype(BF16), vb, _BNN)
    sl = slice(h * HEAD, (h + 1) * HEAD)
    for ci in range(nc):
        r = slice(ci * c, (ci + 1) * c)
        st = st_ref[h]
        o = intra[ci] + lax.dot_general(qd[ci], st.astype(BF16), _NT, preferred_element_type=F32)
        st_ref[h] = st * jnp.exp2(e[ci * c:ci * c + 1, :]) + lax.dot_general(vb[ci], kd[ci], _TN, preferred_element_type=F32)
        o_ref[r, sl] = _head_rms_gate(o, ng, _f32(zg_ref, r, sl))


def _hgrn_kernel(zq_ref, zf_ref, zi_ref, zg_ref, lb_ref, ng_ref, o_ref, st_ref, *, layer):
    @pl.when(pl.program_id(1) == 0)
    def _():
        st_ref[...] = jnp.zeros(st_ref.shape, F32)

    x = lb_ref[...]
    e = jnp.exp(x - jnp.max(x, axis=0, keepdims=True))
    sm = e / jnp.sum(e, axis=0, keepdims=True)
    lb_all = jnp.zeros((1, GROUP), F32)
    for j in range(1, layer + 1):
        lb_all = lb_all + sm[j:j + 1, :]
    ng = ng_ref[...]
    diag, levels = _gla_levels(zq_ref.shape[0])
    rows = slice(None)
    for h in range(N_HEADS):
        sl = slice(h * HEAD, (h + 1) * HEAD)
        lb = lb_all[:, sl]
        f = lb + (1.0 - lb) * _sigmoid(_f32(zf_ref, rows, sl))
        q = _silu(_f32(zq_ref, rows, sl)) * HEAD ** -0.5
        _gla_head(q, 1.0 - f, _f32(zi_ref, rows, sl), jnp.log(f), st_ref, h, zg_ref, ng, o_ref, diag, levels)


def _gla_kernel(zq_ref, zk_ref, zv_ref, zg_ref, zs_ref, w2_ref, gkb_ref, ng_ref, o_ref, st_ref):
    @pl.when(pl.program_id(1) == 0)
    def _():
        st_ref[...] = jnp.zeros(st_ref.shape, F32)

    zs = zs_ref[...]
    lane = lax.broadcasted_iota(jnp.int32, (1, HEAD), 1)
    ng = ng_ref[...]
    diag, levels = _gla_levels(zq_ref.shape[0])
    rows = slice(None)
    for h in range(N_HEADS):
        sl = slice(h * HEAD, (h + 1) * HEAD)
        gk = _log_sigmoid(_mmh(zs, w2_ref[h]) + gkb_ref[h]) / GLA_GATE_NORM
        logf = jnp.where(lane < GLA_DK, gk, 0.0)
        _gla_head(_f32(zq_ref, rows, sl) * GLA_DK ** -0.5, _f32(zk_ref, rows, sl), _f32(zv_ref, rows, sl), logf,
                  st_ref, h, zg_ref, ng, o_ref, diag, levels)


def _gla_scratch(tb):
    return [pltpu.VMEM((N_HEADS, HEAD, HEAD), F32)]


def _hgrn_mixer(z, lower_bounds, norm_g, layer, batch, seq):
    m = z.shape[0]
    tb = _pick(seq, (512, 256, 128, 64))
    per_b = seq // tb
    depth = lower_bounds.shape[0]
    return pl.pallas_call(
        functools.partial(_hgrn_kernel, layer=layer),
        grid=(batch, per_b),
        in_specs=[_zspec(tb, per_b, G_HGRN_Q), _zspec(tb, per_b, G_HGRN_F), _zspec(tb, per_b, G_HGRN_I),
                  _zspec(tb, per_b, G_HGRN_GATE),
                  pl.BlockSpec((depth, GROUP), lambda b, t: (0, 0)),
                  pl.BlockSpec((None, 1, HEAD), lambda b, t: (layer, 0, 0))],
        out_specs=pl.BlockSpec((tb, GROUP), lambda b, t: (b * per_b + t, 0)),
        out_shape=jax.ShapeDtypeStruct((m, GROUP), BF16),
        scratch_shapes=_gla_scratch(tb),
        compiler_params=_params(("parallel", "arbitrary"), 8 * tb * GROUP * 4, 2 * tb * GROUP * 2,
                                4 * tb * GROUP * 4, 8 * tb * GROUP * 4),
        name="hgrn2",
    )(z, z, z, z, lower_bounds, norm_g)


def _gla_mixer(z, zs, w2_pad, gkb_pad, norm_g, layer, batch, seq):
    m = z.shape[0]
    tb = _pick(seq, (512, 256, 128, 64))
    per_b = seq // tb
    return pl.pallas_call(
        _gla_kernel,
        grid=(batch, per_b),
        in_specs=[_zspec(tb, per_b, G_GLA_Q), _zspec(tb, per_b, G_GLA_K), _zspec(tb, per_b, G_GLA_V),
                  _zspec(tb, per_b, G_GLA_GATE),
                  pl.BlockSpec((tb, SMALL), lambda b, t: (b * per_b + t, 0)),
                  pl.BlockSpec((None, N_HEADS, SMALL, HEAD), lambda b, t: (layer, 0, 0, 0)),
                  pl.BlockSpec((None, N_HEADS, 1, HEAD), lambda b, t: (layer, 0, 0, 0)),
                  pl.BlockSpec((None, 1, HEAD), lambda b, t: (layer, 0, 0))],
        out_specs=pl.BlockSpec((tb, GROUP), lambda b, t: (b * per_b + t, 0)),
        out_shape=jax.ShapeDtypeStruct((m, GROUP), BF16),
        scratch_shapes=_gla_scratch(tb),
        compiler_params=_params(("parallel", "arbitrary"), 8 * tb * GROUP * 4, 2 * tb * GROUP * 2,
                                4 * tb * GROUP * 4, 8 * tb * GROUP * 4),
        name="gla",
    )(z, z, z, z, zs, w2_pad, gkb_pad, norm_g)


def _rope_kernel(pos_ref, inv_ref, cos_ref, sin_ref):
    ang = pos_ref[...].astype(F32) * inv_ref[...]
    lane = lax.broadcasted_iota(jnp.int32, ang.shape, 1)
    cos_ref[...] = jnp.cos(ang)
    sin_ref[...] = jnp.where(lane < HEAD // 2, -jnp.sin(ang), jnp.sin(ang))


def _rope_tables(positions):
    m = positions.size
    half = HEAD // 2
    inv = (ROPE_BASE ** (-np.arange(half, dtype=np.float64) / half)).astype(np.float32)
    inv2 = jnp.asarray(np.concatenate([inv, inv])[None, :])
    tb = _pick(m, (1024, 512, 256, 128, 64))
    return pl.pallas_call(
        _rope_kernel,
        grid=(m // tb,),
        in_specs=[pl.BlockSpec((tb, 1), lambda i: (i, 0)), pl.BlockSpec((1, HEAD), lambda i: (0, 0))],
        out_specs=[pl.BlockSpec((tb, HEAD), lambda i: (i, 0)), pl.BlockSpec((tb, HEAD), lambda i: (i, 0))],
        out_shape=[jax.ShapeDtypeStruct((m, HEAD), F32), jax.ShapeDtypeStruct((m, HEAD), F32)],
        compiler_params=_params(("parallel",), 2 * tb * LANE * 4, 4 * tb * HEAD * 4, 4 * tb * HEAD * 4),
        name="rope_tables",
    )(positions.reshape(m, 1), inv2)


def _ret_kernel(zq_ref, zk_ref, zv_ref, zg_ref, cos_ref, sin_ref, ng_ref, nb_ref, o_ref, st_ref, *, tb):
    c = RET_CHUNK

    @pl.when(pl.program_id(1) == 0)
    def _():
        st_ref[...] = jnp.zeros(st_ref.shape, F32)

    ii = lax.broadcasted_iota(jnp.int32, (c, c), 0)
    jj = lax.broadcasted_iota(jnp.int32, (c, c), 1)
    pos = lax.broadcasted_iota(jnp.int32, (c, 1), 0).astype(F32)
    ng, nb = ng_ref[...], nb_ref[...]
    for h in range(N_HEADS):
        sl = slice(h * HEAD, (h + 1) * HEAD)
        lg = LOG_GAMMA[h]
        intra = jnp.where(ii >= jj, jnp.exp((ii - jj).astype(F32) * lg), 0.0)
        xi = jnp.exp((pos + 1.0) * lg)
        zeta = jnp.exp((c - 1.0 - pos) * lg)
        for ci in range(tb // c):
            r = slice(ci * c, (ci + 1) * c)
            cos, sin = cos_ref[r, :], sin_ref[r, :]
            xq, xk = _f32(zq_ref, r, sl), _f32(zk_ref, r, sl)
            q = xq * cos + pltpu.roll(xq, HEAD // 2, axis=1) * sin
            k = (xk * cos + pltpu.roll(xk, HEAD // 2, axis=1) * sin) * HEAD ** -0.5
            v = zv_ref[r, sl]
            st = st_ref[h]
            o = _mm(_mm(q, k, _NT) * intra, v) + _mm(q * xi, st)
            st_ref[h] = math.exp(c * lg) * st + _mm(k * zeta, v, _TN)
            o_ref[r, sl] = _head_group_gate(o, ng, nb, _f32(zg_ref, r, sl))


def _ret_mixer(z, cos2, sin2, norm_g, norm_b, layer, batch, seq):
    m = z.shape[0]
    tb = _pick(seq, (512, 256))
    per_b = seq // tb
    rowspec = pl.BlockSpec((tb, HEAD), lambda b, t: (b * per_b + t, 0))
    prm = pl.BlockSpec((None, 1, HEAD), lambda b, t: (layer, 0, 0))
    return pl.pallas_call(
        functools.partial(_ret_kernel, tb=tb),
        grid=(batch, per_b),
        in_specs=[_zspec(tb, per_b, G_RET_Q), _zspec(tb, per_b, G_RET_K), _zspec(tb, per_b, G_RET_V),
                  _zspec(tb, per_b, G_RET_GATE), rowspec, rowspec, prm, prm],
        out_specs=pl.BlockSpec((tb, GROUP), lambda b, t: (b * per_b + t, 0)),
        out_shape=jax.ShapeDtypeStruct((m, GROUP), BF16),
        scratch_shapes=[pltpu.VMEM((N_HEADS, HEAD, HEAD), F32)],
        compiler_params=_params(("parallel", "arbitrary"), 8 * tb * GROUP * 4, 2 * tb * GROUP * 2,
                                4 * tb * HEAD * 4, 16 * RET_CHUNK * RET_CHUNK * 4),
        name="retention",
    )(z, z, z, z, cos2, sin2, norm_g, norm_b)


def _merge_kernel(u_ref, y0, y1, y2, y3, g0, g1, g2, g3, b0, b1, b2, b3, wb_ref, o_ref):
    u = u_ref[...]
    acc = None
    for m, (y, wg, bg) in enumerate(((y0, g0, b0), (y1, g1, b1), (y2, g2, b2), (y3, g3, b3))):
        gate = _sigmoid(lax.dot_general(u, wg[...], _NN, preferred_element_type=F32) + bg[...])
        term = gate * lax.dot_general(y[...], wb_ref[m], _NN, preferred_element_type=F32)
        acc = term if acc is None else acc + term
    o_ref[...] = acc.astype(BF16)


def _merge(u, ys, w_gate, b_gate, w_branch, layer):
    m, d = u.shape
    tm = _pick(m, (1024, 512, 256, 128))
    td = _pick(d, (512, 256, 128))
    nd = d // td
    yspec = pl.BlockSpec((tm, GROUP), lambda i, j: (i, 0))
    gspecs = [pl.BlockSpec((None, d, td), functools.partial(lambda i, j, br: (layer, 0, br * nd + j), br=br))
              for br in range(N_BRANCH)]
    bspecs = [pl.BlockSpec((None, 1, td), functools.partial(lambda i, j, br: (layer, 0, br * nd + j), br=br))
              for br in range(N_BRANCH)]
    return pl.pallas_call(
        _merge_kernel,
        grid=(m // tm, nd),
        in_specs=[pl.BlockSpec((tm, d), lambda i, j: (i, 0))] + [yspec] * N_BRANCH + gspecs + bspecs
                 + [pl.BlockSpec((None, N_BRANCH, GROUP, td), lambda i, j: (layer, 0, 0, j))],
        out_specs=pl.BlockSpec((tm, td), lambda i, j: (i, j)),
        out_shape=jax.ShapeDtypeStruct((m, d), BF16),
        compiler_params=_params(("parallel", "arbitrary"), 2 * tm * d * 2, 2 * N_BRANCH * tm * GROUP * 2,
                                2 * N_BRANCH * d * td * 2, 2 * N_BRANCH * GROUP * td * 2, 2 * tm * td * 2,
                                6 * tm * td * 4),
        name="branch_merge",
    )(u, *ys, *([w_gate] * N_BRANCH), *([b_gate] * N_BRANCH), w_branch)


def _outproj_kernel(x_ref, w_ref, h_ref, gate_ref, o_ref):
    o_ref[...] = h_ref[...] + gate_ref[...] * lax.dot_general(x_ref[...], w_ref[...], _NN, preferred_element_type=F32)


def _out_projection(x, w_out, h, gate, layer, seq):
    m, d = h.shape
    tm = _pick(seq, (512, 256, 128))
    per_b = seq // tm
    return pl.pallas_call(
        _outproj_kernel,
        grid=(m // tm,),
        in_specs=[pl.BlockSpec((tm, d), lambda i: (i, 0)),
                  pl.BlockSpec((None, d, d), lambda i: (layer, 0, 0)),
                  pl.BlockSpec((tm, d), lambda i: (i, 0)),
                  pl.BlockSpec((None, 1, d), lambda i: (i // per_b, 0, 0))],
        out_specs=pl.BlockSpec((tm, d), lambda i: (i, 0)),
        out_shape=jax.ShapeDtypeStruct((m, d), F32),
        compiler_params=_params(("parallel",), 2 * tm * d * 2, 2 * d * d * 2, 4 * tm * d * 4, 2 * tm * d * 4),
        name="out_projection",
    )(x, w_out, h, gate)


def _ffn_kernel(h_ref, g_ref, sc_ref, sh_ref, gt_ref, wg_ref, wu_ref, wd_ref, fg_ref, o_ref, u_s, *, final):
    f = pl.program_id(1)

    @pl.when(f == 0)
    def _():
        u_s[...] = _norm_mod(h_ref[...], g_ref[...], sc_ref[...], sh_ref[...])
        o_ref[...] = jnp.zeros(o_ref.shape, F32)

    u = u_s[...]
    a = lax.dot_general(u, wg_ref[...], _NN, preferred_element_type=F32)
    b = lax.dot_general(u, wu_ref[...], _NN, preferred_element_type=F32)
    o_ref[...] += lax.dot_general((_silu(a) * b).astype(BF16), wd_ref[...], _NN, preferred_element_type=F32)

    @pl.when(f == pl.num_programs(1) - 1)
    def _():
        hn = h_ref[...] + gt_ref[...] * o_ref[...]
        o_ref[...] = _rms(hn, fg_ref[...]) if final else hn


def _ffn(h, gain, scale, shift, gate, w_gate, w_up, w_down, final_gain, layer, seq, final):
    m, d = h.shape
    dff = w_gate.shape[-1]
    tm = _pick(seq, (512, 256, 128))
    tf = _pick(dff, (512, 256, 128))
    per_b = seq // tm
    mod = pl.BlockSpec((None, 1, d), lambda i, f: (i // per_b, 0, 0))
    return pl.pallas_call(
        functools.partial(_ffn_kernel, final=final),
        grid=(m // tm, dff // tf),
        in_specs=[pl.BlockSpec((tm, d), lambda i, f: (i, 0)),
                  pl.BlockSpec((None, 1, d), lambda i, f: (layer, 0, 0)),
                  mod, mod, mod,
                  pl.BlockSpec((None, d, tf), lambda i, f: (layer, 0, f)),
                  pl.BlockSpec((None, d, tf), lambda i, f: (layer, 0, f)),
                  pl.BlockSpec((None, tf, d), lambda i, f: (layer, f, 0)),
                  pl.BlockSpec((1, d), lambda i, f: (0, 0))],
        out_specs=pl.BlockSpec((tm, d), lambda i, f: (i, 0)),
        out_shape=jax.ShapeDtypeStruct((m, d), F32),
        scratch_shapes=[pltpu.VMEM((tm, d), BF16)],
        compiler_params=_params(("parallel", "arbitrary"), 4 * tm * d * 4, 6 * d * tf * 2, tm * d * 2,
                                4 * tm * tf * 4),
        name="swiglu_ffn",
    )(h, gain, scale, shift, gate, w_gate, w_up, w_down, final_gain)


def _split_offsets():
    widths = (("gdn_q", GROUP), ("gdn_k", GROUP), ("gdn_v", GROUP), ("gdn_a", N_HEADS), ("gdn_b", N_HEADS),
              ("gdn_gate", GROUP), ("hgrn_q", GROUP), ("hgrn_f", GROUP), ("hgrn_i", GROUP), ("hgrn_gate", GROUP),
              ("ret_q", GROUP), ("ret_k", GROUP), ("ret_v", GROUP), ("ret_gate", GROUP),
              ("gla_q", N_HEADS * GLA_DK), ("gla_k", N_HEADS * GLA_DK), ("gla_v", GROUP),
              ("gla_gk_lr", GLA_LOWRANK), ("gla_gate", GROUP))
    out, off = {}, 0
    for name, w in widths:
        out[name] = (off, w)
        off += w
    return out, off


def _permute_w_in(w_in):
    offs, total = _split_offsets()
    assert w_in.shape[-1] == total
    depth, d, _ = w_in.shape
    w_in = w_in.astype(BF16)

    def cols(name):
        o, w = offs[name]
        return w_in[:, :, o:o + w]

    def pad_heads(name):
        x = cols(name).reshape(depth, d, N_HEADS, GLA_DK)
        return jnp.pad(x, ((0, 0), (0, 0), (0, 0), (0, HEAD - GLA_DK))).reshape(depth, d, GROUP)

    big = [cols("gdn_q"), cols("gdn_k"), cols("gdn_v"), cols("gdn_gate"),
           cols("hgrn_q"), cols("hgrn_f"), cols("hgrn_i"), cols("hgrn_gate"),
           cols("ret_q"), cols("ret_k"), cols("ret_v"), cols("ret_gate"),
           pad_heads("gla_q"), pad_heads("gla_k"), cols("gla_v"), cols("gla_gate")]
    small = jnp.concatenate([cols("gdn_a"), cols("gdn_b"), cols("gla_gk_lr")], axis=-1)
    small = jnp.pad(small, ((0, 0), (0, 0), (0, SMALL - small.shape[-1])))
    return jnp.concatenate(big, axis=-1), small


def _lane_row(x, width=SMALL):
    return jnp.pad(x.astype(F32), ((0, 0), (0, width - x.shape[-1])))[:, None, :]


def kernel(x, c, positions, ada_w, ada_b, norm_mix_g, w_in, gdn_conv_w, gdn_a_log, gdn_dt_bias, gdn_norm_g, hgrn_lower_bounds, hgrn_norm_g, ret_norm_g, ret_norm_b, gla_gk_w2, gla_gk_b, gla_norm_g, w_branch, w_merge_gate, b_merge_gate, w_out, norm_ffn_g, ffn_w_gate, ffn_w_up, ffn_w_down, final_norm_g):
    batch, seq, d = x.shape
    depth = w_in.shape[0]
    m = batch * seq
    assert seq % max(RET_CHUNK, GDN_CHUNK) == 0 and d % LANE == 0

    w_big, w_small = _permute_w_in(w_in)
    w_mg = w_merge_gate.astype(BF16)
    b_mg = b_merge_gate.reshape(depth, 1, N_BRANCH * d)
    w_br = w_branch.astype(BF16)
    w_o = w_out.astype(BF16)
    w_fg, w_fu, w_fd = ffn_w_gate.astype(BF16), ffn_w_up.astype(BF16), ffn_w_down.astype(BF16)
    alog_row, dt_row = _lane_row(gdn_a_log), _lane_row(gdn_dt_bias)
    w2 = gla_gk_w2.reshape(depth, GLA_LOWRANK, N_HEADS, GLA_DK).transpose(0, 2, 1, 3)
    w2_pad = jnp.zeros((depth, N_HEADS, SMALL, HEAD), F32).at[:, :, 2 * N_HEADS:2 * N_HEADS + GLA_LOWRANK, :GLA_DK].set(w2)
    gkb_pad = jnp.pad(gla_gk_b.reshape(depth, N_HEADS, 1, GLA_DK), ((0, 0), (0, 0), (0, 0), (0, HEAD - GLA_DK)))
    row3 = lambda a: a.reshape(depth, 1, a.shape[-1])

    mod = _modulation(c, ada_w, ada_b)
    cos2, sin2 = _rope_tables(positions)

    final_gain = final_norm_g.reshape(1, d)
    h = x.reshape(m, d)
    for l in range(depth):
        shift_m, scale_m, gate_m, shift_f, scale_f, gate_f = [mod[l, :, i * d:(i + 1) * d][:, None, :] for i in range(6)]
        u, z, zs = _in_projection(h, row3(norm_mix_g), scale_m, shift_m, w_big, w_small, l, seq)
        ys = (_gdn_mixer(z, zs, gdn_conv_w, alog_row, dt_row, row3(gdn_norm_g), l, batch, seq),
              _hgrn_mixer(z, hgrn_lower_bounds, row3(hgrn_norm_g), l, batch, seq),
              _ret_mixer(z, cos2, sin2, row3(ret_norm_g), row3(ret_norm_b), l, batch, seq),
              _gla_mixer(z, zs, w2_pad, gkb_pad, row3(gla_norm_g), l, batch, seq))
        merged = _merge(u, ys, w_mg, b_mg, w_br, l)
        h = _out_projection(merged, w_o, h, gate_m, l, seq)
        h = _ffn(h, row3(norm_ffn_g), scale_f, shift_f, gate_f, w_fg, w_fu, w_fd, final_gain, l, seq,
                 final=(l == depth - 1))
    return h.reshape(batch, seq, d)
```

```python
import functools
import math

import numpy as np
import jax
import jax.numpy as jnp
from jax import lax
from jax.experimental import pallas as pl
from jax.experimental.pallas import tpu as pltpu

F32 = jnp.float32
BF16 = jnp.bfloat16
HI = lax.Precision.HIGHEST

N_HEADS = 4
HEAD = 128
GROUP = N_HEADS * HEAD
GLA_DK = 64
GLA_LOWRANK = 16
GLA_GATE_NORM = 16.0
CONV_W = 4
EPS = 1e-6
ROPE_BASE = 10000.0
N_BRANCH = 4

GDN_CHUNK = 64
GLA_CHUNK = 128
RET_CHUNK = 256
SMALL = 128

LANE = 128
SUBLANE = 8
VMEM_CAP_BYTES = 60000 * 1024
INTERNAL_SCRATCH_BYTES = 12 * 1024 * 1024

G_GDN_Q, G_GDN_K, G_GDN_V, G_GDN_GATE = 0, 1, 2, 3
G_HGRN_Q, G_HGRN_F, G_HGRN_I, G_HGRN_GATE = 4, 5, 6, 7
G_RET_Q, G_RET_K, G_RET_V, G_RET_GATE = 8, 9, 10, 11
G_GLA_Q, G_GLA_K, G_GLA_V, G_GLA_GATE = 12, 13, 14, 15
N_GROUPS = 16

LOG2_E = 1.4426950408889634
LOG_GAMMA = tuple(float(v) for v in np.log1p(-np.exp(np.linspace(math.log(1.0 / 32.0), math.log(1.0 / 512.0), N_HEADS))))

_NN = (((1,), (0,)), ((), ()))
_NT = (((1,), (1,)), ((), ()))
_TN = (((0,), (0,)), ((), ()))


def _mm(a, b, dims=_NN):
    return lax.dot_general(a.astype(BF16), b.astype(BF16), dims, preferred_element_type=F32)


def _mmh(a, b, dims=_NN):
    return lax.dot_general(a, b, dims, precision=HI, preferred_element_type=F32)


def _sigmoid(x):
    return 1.0 / (1.0 + jnp.exp(-x))


def _silu(x):
    return x * _sigmoid(x)


def _softplus(x):
    return jnp.maximum(x, 0.0) + jnp.log1p(jnp.exp(-jnp.abs(x)))


def _log_sigmoid(x):
    return -_softplus(-x)


def _rms(x, g):
    return x * lax.rsqrt(jnp.mean(x * x, axis=-1, keepdims=True) + EPS) * g


def _seg_cumsum(x, seg):
    row = lax.broadcasted_iota(jnp.int32, x.shape, 0) & (seg - 1)
    s = 1
    while s < seg:
        x = x + jnp.where(row >= s, pltpu.roll(x, s, axis=0), 0.0)
        s *= 2
    return x


def _pick(n, prefs):
    for p in prefs:
        if n % p == 0:
            return p
    return n


def _params(sem, *nbytes):
    need = int(sum(nbytes)) + INTERNAL_SCRATCH_BYTES
    return pltpu.CompilerParams(dimension_semantics=sem, vmem_limit_bytes=min(VMEM_CAP_BYTES, need))


def _mod_kernel(c_ref, w_ref, b_ref, o_ref):
    o_ref[...] = _mm(_silu(c_ref[...]), w_ref[...]) + b_ref[...]


def _modulation(c, ada_w, ada_b):
    depth, d, n = ada_w.shape
    b = c.shape[0]
    rows = -(-b // SUBLANE) * SUBLANE
    c_pad = jnp.zeros((rows, d), F32).at[:b].set(c)
    tn = _pick(n, (1536, 1024, 512, 256, 128))
    out = pl.pallas_call(
        _mod_kernel,
        grid=(depth, n // tn),
        in_specs=[pl.BlockSpec((rows, d), lambda l, j: (0, 0)),
                  pl.BlockSpec((None, d, tn), lambda l, j: (l, 0, j)),
                  pl.BlockSpec((None, 1, tn), lambda l, j: (l, 0, j))],
        out_specs=pl.BlockSpec((None, rows, tn), lambda l, j: (l, 0, j)),
        out_shape=jax.ShapeDtypeStruct((depth, rows, n), F32),
        compiler_params=_params(("parallel", "parallel"), 2 * d * tn * 4, 4 * rows * tn * 4, 8 * rows * d * 4),
        name="adaln_modulation",
    )(c_pad, ada_w, ada_b.reshape(depth, 1, n))
    return out[:, :b]


def _norm_mod(x, g, scale, shift):
    return (_rms(x, g) * (1.0 + scale) + shift).astype(BF16)


def _inproj_kernel(h_ref, g_ref, sc_ref, sh_ref, w_ref, ws_ref, u_ref, z_ref, zs_ref):
    @pl.when(pl.program_id(1) == 0)
    def _():
        u = _norm_mod(h_ref[...], g_ref[...], sc_ref[...], sh_ref[...])
        u_ref[...] = u
        zs_ref[...] = lax.dot_general(u, ws_ref[...], _NN, preferred_element_type=F32)

    z_ref[...] = lax.dot_general(u_ref[...], w_ref[...], _NN, preferred_element_type=F32).astype(z_ref.dtype)


def _in_projection(h, gain, scale, shift, w_big, w_small, layer, seq):
    m, d = h.shape
    nb = w_big.shape[-1]
    tm = _pick(seq, (1024, 512, 256, 128))
    tn = _pick(nb, (1024, 512, 256, 128))
    per_b = seq // tm
    return pl.pallas_call(
        _inproj_kernel,
        grid=(m // tm, nb // tn),
        in_specs=[pl.BlockSpec((tm, d), lambda i, j: (i, 0)),
                  pl.BlockSpec((None, 1, d), lambda i, j: (layer, 0, 0)),
                  pl.BlockSpec((None, 1, d), lambda i, j: (i // per_b, 0, 0)),
                  pl.BlockSpec((None, 1, d), lambda i, j: (i // per_b, 0, 0)),
                  pl.BlockSpec((None, d, tn), lambda i, j: (layer, 0, j)),
                  pl.BlockSpec((None, d, SMALL), lambda i, j: (layer, 0, 0))],
        out_specs=[pl.BlockSpec((tm, d), lambda i, j: (i, 0)),
                   pl.BlockSpec((tm, tn), lambda i, j: (i, j)),
                   pl.BlockSpec((tm, SMALL), lambda i, j: (i, 0))],
        out_shape=[jax.ShapeDtypeStruct((m, d), BF16),
                   jax.ShapeDtypeStruct((m, nb), BF16),
                   jax.ShapeDtypeStruct((m, SMALL), F32)],
        compiler_params=_params(("parallel", "arbitrary"), 2 * tm * d * 4, 2 * tm * d * 2, 2 * d * tn * 2,
                                2 * tm * tn * 2, 2 * d * SMALL * 2, 2 * tm * SMALL * 4, 2 * tm * d * 4),
        name="norm_in_projection",
    )(h, gain, scale, shift, w_big, w_small)


def _head_rms_gate(o, g, gate):
    return (o * lax.rsqrt(jnp.mean(o * o, axis=-1, keepdims=True) + EPS) * g * _silu(gate)).astype(BF16)


def _head_group_gate(o, g, b, gate):
    mu = jnp.mean(o, axis=-1, keepdims=True)
    oc = o - mu
    var = jnp.mean(oc * oc, axis=-1, keepdims=True)
    return ((oc * lax.rsqrt(var + EPS) * g + b) * _silu(gate)).astype(BF16)


def _f32(ref, rows, cols):
    return ref[rows, cols].astype(F32)


def _zspec(tb, per_b, group):
    return pl.BlockSpec((tb, GROUP), lambda b, t: (b * per_b + t, group))


def _causal_conv(x, prev, w):
    acc = x * w[CONV_W - 1:CONV_W]
    rows = lax.broadcasted_iota(jnp.int32, prev.shape, 0)
    for s in range(1, CONV_W):
        xs = pltpu.roll(x, s, axis=0)
        head = jnp.where(rows < s, pltpu.roll(prev, s, axis=0), xs[0:SUBLANE])
        xs = jnp.concatenate([head, xs[SUBLANE:]], axis=0)
        acc = acc + xs * w[CONV_W - 1 - s:CONV_W - s]
    return acc


def _l2norm(x):
    return x * lax.rsqrt(jnp.sum(x * x, axis=-1, keepdims=True) + EPS)


def _split_bf16(a):
    hi = a.astype(BF16)
    return hi, (a - hi.astype(F32)).astype(BF16)


_BNN = (((2,), (1,)), ((0,), (0,)))
_BNT = (((2,), (2,)), ((0,), (0,)))
_BTN = (((1,), (1,)), ((0,), (0,)))


def _bdot(a, b, dims):
    return lax.dot_general(a, b, dims, preferred_element_type=F32)


def _bmm3(a, b, dims=_BNN):
    (ah, al), (bh, bl) = a, b
    return _bdot(ah, bh, dims) + _bdot(ah, bl, dims) + _bdot(al, bh, dims)


def _unit_lower_inverse(lm, ii, jj):
    c = lm.shape[-1]
    x = jnp.where(ii == jj, 1.0, 0.0) - jnp.where(((ii & 1) == 1) & (jj == ii - 1), lm, 0.0)
    lms = _split_bf16(lm)
    k = 1
    while (1 << k) < c:
        same = (ii >> (k + 1)) == (jj >> (k + 1))
        sel = same & (((ii >> k) & 1) == 1) & (((jj >> k) & 1) == 0)
        xs = _split_bf16(x)
        x = x - jnp.where(sel, _bmm3(_split_bf16(_bmm3(xs, lms)), xs), 0.0)
        k += 1
    return x


def _gdn_kernel(zq_ref, zk_ref, zv_ref, zg_ref, zs_ref, cw_ref, alog_ref, dt_ref, ng_ref, o_ref,
                prev_ref, q_s, k_s, v_s, gb_s, u_s, p_s, n_s, st_ref, *, tb):
    c = GDN_CHUNK
    nc = tb // c

    @pl.when(pl.program_id(1) == 0)
    def _():
        prev_ref[...] = jnp.zeros(prev_ref.shape, F32)
        st_ref[...] = jnp.zeros(st_ref.shape, F32)

    cw = cw_ref[...]
    for idx, (src, dst) in enumerate(((zq_ref, q_s), (zk_ref, k_s), (zv_ref, v_s))):
        x = src[...].astype(F32)
        y = _silu(_causal_conv(x, prev_ref[idx], cw[:, idx * GROUP:(idx + 1) * GROUP]))
        prev_ref[idx] = x[tb - SUBLANE:tb]
        for h in range(N_HEADS):
            yh = y[:, h * HEAD:(h + 1) * HEAD]
            if idx == 0:
                yh = _l2norm(yh) * HEAD ** -0.5
            elif idx == 1:
                yh = _l2norm(yh)
            for ci in range(nc):
                dst[h, ci] = yh[ci * c:(ci + 1) * c]

    zs = zs_ref[...]
    lane = lax.broadcasted_iota(jnp.int32, (1, SMALL), 1)
    g = -jnp.exp(alog_ref[...]) * _softplus(zs + dt_ref[...])
    gcum = _seg_cumsum(jnp.where(lane < N_HEADS, g, 0.0), c)
    gball = jnp.where(lane < N_HEADS, gcum, _sigmoid(zs))
    for ci in range(nc):
        gb_s[ci] = gball[ci * c:(ci + 1) * c]

    ii = lax.broadcasted_iota(jnp.int32, (1, c, c), 1)
    jj = lax.broadcasted_iota(jnp.int32, (1, c, c), 2)
    onehot = jnp.where(lax.broadcasted_iota(jnp.int32, (nc, SUBLANE, SMALL), 1)
                       == lax.broadcasted_iota(jnp.int32, (nc, SUBLANE, SMALL), 2), 1.0, 0.0)
    gb = gb_s[...]
    grow = lax.dot_general(onehot, gb, _BNT, precision=HI, preferred_element_type=F32)
    for h in range(N_HEADS):
        q, k, v = q_s[h], k_s[h], v_s[h]
        gcol = gb[:, :, h:h + 1]
        beta = gb[:, :, N_HEADS + h:N_HEADS + h + 1]
        dec = jnp.exp(jnp.where(ii >= jj, gcol - grow[:, h:h + 1, :], -jnp.inf))
        kb = k * beta
        kbf = k.astype(BF16)
        lm = jnp.where(ii > jj, _bdot(kb.astype(BF16), kbf, _BNT) * dec, 0.0)
        tinv = _unit_lower_inverse(lm, ii, jj)
        egc = jnp.exp(gcol)
        wu = _bmm3(_split_bf16(tinv), _split_bf16(jnp.concatenate([kb * egc, v * beta], axis=2))).astype(BF16)
        a = (_bdot(q.astype(BF16), kbf, _BNT) * dec).astype(BF16)
        kd = (k * jnp.exp(gcol[:, c - 1:c, :] - gcol)).astype(BF16)
        aw = _bdot(a, wu, _BNN)
        kw = _bdot(kd, wu, _BTN)
        q_s[h] = q * egc - aw[:, :, :HEAD]
        u_s[h] = aw[:, :, HEAD:]
        p_s[h] = kw[:, :, :HEAD]
        n_s[h] = kw[:, :, HEAD:]

    ng = ng_ref[...]
    for ci in range(nc):
        r = slice(ci * c, (ci + 1) * c)
        for h in range(N_HEADS):
            sl = slice(h * HEAD, (h + 1) * HEAD)
            st = st_ref[h]
            o = u_s[h, ci] + _mm(q_s[h, ci], st)
            st_ref[h] = jnp.exp(gb_s[ci, c - 1:c, h:h + 1]) * st - _mm(p_s[h, ci], st) + n_s[h, ci]
            o_ref[r, sl] = _head_rms_gate(o, ng, _f32(zg_ref, r, sl))


def _gdn_mixer(z, zs, conv_w, alog_row, dt_row, norm_g, layer, batch, seq):
    m = z.shape[0]
    tb = _pick(seq, (512, 256, 128, 64))
    per_b = seq // tb
    nc = tb // GDN_CHUNK
    nconv = conv_w.shape[-1]
    row = lambda a: pl.BlockSpec((None, 1, a.shape[-1]), lambda b, t: (layer, 0, 0))
    return pl.pallas_call(
        functools.partial(_gdn_kernel, tb=tb),
        grid=(batch, per_b),
        in_specs=[_zspec(tb, per_b, G_GDN_Q), _zspec(tb, per_b, G_GDN_K), _zspec(tb, per_b, G_GDN_V),
                  _zspec(tb, per_b, G_GDN_GATE),
                  pl.BlockSpec((tb, SMALL), lambda b, t: (b * per_b + t, 0)),
                  pl.BlockSpec((None, CONV_W, nconv), lambda b, t: (layer, 0, 0)),
                  row(alog_row), row(dt_row), row(norm_g)],
        out_specs=pl.BlockSpec((tb, GROUP), lambda b, t: (b * per_b + t, 0)),
        out_shape=jax.ShapeDtypeStruct((m, GROUP), BF16),
        scratch_shapes=[pltpu.VMEM((3, SUBLANE, GROUP), F32)]
                       + [pltpu.VMEM((N_HEADS, nc, GDN_CHUNK, HEAD), F32) for _ in range(3)]
                       + [pltpu.VMEM((nc, GDN_CHUNK, SMALL), F32)]
                       + [pltpu.VMEM((N_HEADS, nc, GDN_CHUNK, HEAD), F32)]
                       + [pltpu.VMEM((N_HEADS, nc, HEAD, HEAD), F32) for _ in range(2)]
                       + [pltpu.VMEM((N_HEADS, HEAD, HEAD), F32)],
        compiler_params=_params(("parallel", "arbitrary"), 8 * tb * GROUP * 4, 2 * tb * GROUP * 2,
                                5 * tb * GROUP * 4, 2 * N_HEADS * nc * HEAD * HEAD * 4, 8 * tb * GROUP * 4),
        name="gated_deltanet",
    )(z, z, z, z, zs, conv_w, alog_row, dt_row, norm_g)


def _gla_levels(tb):
    c = GLA_CHUNK
    row = lax.broadcasted_iota(jnp.int32, (tb, HEAD), 0)
    ii = lax.broadcasted_iota(jnp.int32, (1, c, c), 1)
    jj = lax.broadcasted_iota(jnp.int32, (1, c, c), 2)
    levels = []
    hs, bit = 1, 0
    while hs < c:
        pair = ((ii >> (bit + 1)) == (jj >> (bit + 1))) & ((ii & hs) != 0) & ((jj & hs) == 0)
        levels.append((hs, (row & hs) != 0, pair))
        hs, bit = hs * 2, bit + 1
    return ii == jj, levels


def _gla_head(q, k, v, logf, st_ref, h, zg_ref, ng, o_ref, diag, levels, pending):
    tb = q.shape[0]
    c = GLA_CHUNK
    nc = tb // c
    chunks = lambda a: a.reshape(nc, c, HEAD).astype(BF16)

    b = _seg_cumsum(logf * LOG2_E, c)
    e = b
    scores = jnp.where(diag, _bdot(chunks(q), chunks(k), _BNT), 0.0)
    for hs, upper, pair in levels:
        expo = jnp.minimum(jnp.where(upper, b - pltpu.roll(e, hs, axis=0), e - b), 0.0)
        x = chunks(jnp.where(upper, q, k) * jnp.exp2(expo))
        scores = jnp.where(pair, _bdot(x, x, _BNT), scores)
        e = jnp.where(upper, e, pltpu.roll(e, tb - hs, axis=0))
        if pending:
            pending.pop(0)()
    qd = chunks(q * jnp.exp2(b))
    kd = chunks(k * jnp.exp2(e - b))
    vb = chunks(v)
    intra = _bdot(scores.astype(BF16), vb, _BNN)
    sl = slice(h * HEAD, (h + 1) * HEAD)

    def step(ci):
        r = slice(ci * c, (ci + 1) * c)
        st = st_ref[h]
        o = intra[ci] + lax.dot_general(qd[ci], st.astype(BF16), _NT, preferred_element_type=F32)
        st_ref[h] = st * jnp.exp2(e[ci * c:ci * c + 1, :]) + lax.dot_general(vb[ci], kd[ci], _TN, preferred_element_type=F32)
        o_ref[r, sl] = _head_rms_gate(o, ng, _f32(zg_ref, r, sl))

    while pending:
        pending.pop(0)()
    return [functools.partial(step, ci) for ci in range(nc)]


def _hgrn_kernel(zq_ref, zf_ref, zi_ref, zg_ref, lb_ref, ng_ref, o_ref, st_ref, *, layer):
    @pl.when(pl.program_id(1) == 0)
    def _():
        st_ref[...] = jnp.zeros(st_ref.shape, F32)

    x = lb_ref[...]
    e = jnp.exp(x - jnp.max(x, axis=0, keepdims=True))
    sm = e / jnp.sum(e, axis=0, keepdims=True)
    lb_all = jnp.zeros((1, GROUP), F32)
    for j in range(1, layer + 1):
        lb_all = lb_all + sm[j:j + 1, :]
    ng = ng_ref[...]
    diag, levels = _gla_levels(zq_ref.shape[0])
    rows = slice(None)
    pending = []
    for h in range(N_HEADS):
        sl = slice(h * HEAD, (h + 1) * HEAD)
        lb = lb_all[:, sl]
        f = lb + (1.0 - lb) * _sigmoid(_f32(zf_ref, rows, sl))
        q = _silu(_f32(zq_ref, rows, sl)) * HEAD ** -0.5
        pending = _gla_head(q, 1.0 - f, _f32(zi_ref, rows, sl), jnp.log(f), st_ref, h, zg_ref, ng, o_ref,
                            diag, levels, pending)
    for step in pending:
        step()


def _gla_kernel(zq_ref, zk_ref, zv_ref, zg_ref, zs_ref, w2_ref, gkb_ref, ng_ref, o_ref, st_ref):
    @pl.when(pl.program_id(1) == 0)
    def _():
        st_ref[...] = jnp.zeros(st_ref.shape, F32)

    lane = lax.broadcasted_iota(jnp.int32, (1, HEAD), 1)
    ng = ng_ref[...]
    diag, levels = _gla_levels(zq_ref.shape[0])
    rows = slice(None)
    gk_all = _log_sigmoid(_mm(zs_ref[...], w2_ref[...]) + gkb_ref[...]) / GLA_GATE_NORM
    pending = []
    for h in range(N_HEADS):
        sl = slice(h * HEAD, (h + 1) * HEAD)
        logf = jnp.where(lane < GLA_DK, gk_all[:, sl], 0.0)
        pending = _gla_head(_f32(zq_ref, rows, sl) * GLA_DK ** -0.5, _f32(zk_ref, rows, sl), _f32(zv_ref, rows, sl),
                            logf, st_ref, h, zg_ref, ng, o_ref, diag, levels, pending)
    for step in pending:
        step()


def _gla_scratch(tb):
    return [pltpu.VMEM((N_HEADS, HEAD, HEAD), F32)]


def _hgrn_mixer(z, lower_bounds, norm_g, layer, batch, seq):
    m = z.shape[0]
    tb = _pick(seq, (512, 256, 128, 64))
    per_b = seq // tb
    depth = lower_bounds.shape[0]
    return pl.pallas_call(
        functools.partial(_hgrn_kernel, layer=layer),
        grid=(batch, per_b),
        in_specs=[_zspec(tb, per_b, G_HGRN_Q), _zspec(tb, per_b, G_HGRN_F), _zspec(tb, per_b, G_HGRN_I),
                  _zspec(tb, per_b, G_HGRN_GATE),
                  pl.BlockSpec((depth, GROUP), lambda b, t: (0, 0)),
                  pl.BlockSpec((None, 1, HEAD), lambda b, t: (layer, 0, 0))],
        out_specs=pl.BlockSpec((tb, GROUP), lambda b, t: (b * per_b + t, 0)),
        out_shape=jax.ShapeDtypeStruct((m, GROUP), BF16),
        scratch_shapes=_gla_scratch(tb),
        compiler_params=_params(("parallel", "arbitrary"), 8 * tb * GROUP * 4, 2 * tb * GROUP * 2,
                                4 * tb * GROUP * 4, 8 * tb * GROUP * 4),
        name="hgrn2",
    )(z, z, z, z, lower_bounds, norm_g)


def _gla_mixer(z, zs, w2_pad, gkb_pad, norm_g, layer, batch, seq):
    m = z.shape[0]
    tb = _pick(seq, (512, 256, 128, 64))
    per_b = seq // tb
    return pl.pallas_call(
        _gla_kernel,
        grid=(batch, per_b),
        in_specs=[_zspec(tb, per_b, G_GLA_Q), _zspec(tb, per_b, G_GLA_K), _zspec(tb, per_b, G_GLA_V),
                  _zspec(tb, per_b, G_GLA_GATE),
                  pl.BlockSpec((tb, SMALL), lambda b, t: (b * per_b + t, 0)),
                  pl.BlockSpec((None, SMALL, GROUP), lambda b, t: (layer, 0, 0)),
                  pl.BlockSpec((None, 1, GROUP), lambda b, t: (layer, 0, 0)),
                  pl.BlockSpec((None, 1, HEAD), lambda b, t: (layer, 0, 0))],
        out_specs=pl.BlockSpec((tb, GROUP), lambda b, t: (b * per_b + t, 0)),
        out_shape=jax.ShapeDtypeStruct((m, GROUP), BF16),
        scratch_shapes=_gla_scratch(tb),
        compiler_params=_params(("parallel", "arbitrary"), 8 * tb * GROUP * 4, 2 * tb * GROUP * 2,
                                4 * tb * GROUP * 4, 8 * tb * GROUP * 4),
        name="gla",
    )(z, z, z, z, zs, w2_pad, gkb_pad, norm_g)


def _rope_kernel(pos_ref, inv_ref, cos_ref, sin_ref):
    ang = pos_ref[...].astype(F32) * inv_ref[...]
    lane = lax.broadcasted_iota(jnp.int32, ang.shape, 1)
    cos_ref[...] = jnp.cos(ang)
    sin_ref[...] = jnp.where(lane < HEAD // 2, -jnp.sin(ang), jnp.sin(ang))


def _rope_tables(positions):
    m = positions.size
    half = HEAD // 2
    inv = (ROPE_BASE ** (-np.arange(half, dtype=np.float64) / half)).astype(np.float32)
    inv2 = jnp.asarray(np.concatenate([inv, inv])[None, :])
    tb = _pick(m, (1024, 512, 256, 128, 64))
    return pl.pallas_call(
        _rope_kernel,
        grid=(m // tb,),
        in_specs=[pl.BlockSpec((tb, 1), lambda i: (i, 0)), pl.BlockSpec((1, HEAD), lambda i: (0, 0))],
        out_specs=[pl.BlockSpec((tb, HEAD), lambda i: (i, 0)), pl.BlockSpec((tb, HEAD), lambda i: (i, 0))],
        out_shape=[jax.ShapeDtypeStruct((m, HEAD), F32), jax.ShapeDtypeStruct((m, HEAD), F32)],
        compiler_params=_params(("parallel",), 2 * tb * LANE * 4, 4 * tb * HEAD * 4, 4 * tb * HEAD * 4),
        name="rope_tables",
    )(positions.reshape(m, 1), inv2)


def _ret_kernel(zq_ref, zk_ref, zv_ref, zg_ref, cos_ref, sin_ref, ng_ref, nb_ref, o_ref, st_ref, *, tb):
    c = RET_CHUNK

    @pl.when(pl.program_id(1) == 0)
    def _():
        st_ref[...] = jnp.zeros(st_ref.shape, F32)

    ii = lax.broadcasted_iota(jnp.int32, (c, c), 0)
    jj = lax.broadcasted_iota(jnp.int32, (c, c), 1)
    pos = lax.broadcasted_iota(jnp.int32, (c, 1), 0).astype(F32)
    ng, nb = ng_ref[...], nb_ref[...]
    for h in range(N_HEADS):
        sl = slice(h * HEAD, (h + 1) * HEAD)
        lg = LOG_GAMMA[h]
        intra = jnp.where(ii >= jj, jnp.exp((ii - jj).astype(F32) * lg), 0.0)
        xi = jnp.exp((pos + 1.0) * lg)
        zeta = jnp.exp((c - 1.0 - pos) * lg)
        for ci in range(tb // c):
            r = slice(ci * c, (ci + 1) * c)
            cos, sin = cos_ref[r, :], sin_ref[r, :]
            xq, xk = _f32(zq_ref, r, sl), _f32(zk_ref, r, sl)
            q = xq * cos + pltpu.roll(xq, HEAD // 2, axis=1) * sin
            k = (xk * cos + pltpu.roll(xk, HEAD // 2, axis=1) * sin) * HEAD ** -0.5
            v = zv_ref[r, sl]
            st = st_ref[h]
            o = _mm(_mm(q, k, _NT) * intra, v) + _mm(q * xi, st)
            st_ref[h] = math.exp(c * lg) * st + _mm(k * zeta, v, _TN)
            o_ref[r, sl] = _head_group_gate(o, ng, nb, _f32(zg_ref, r, sl))


def _ret_mixer(z, cos2, sin2, norm_g, norm_b, layer, batch, seq):
    m = z.shape[0]
    tb = _pick(seq, (512, 256))
    per_b = seq // tb
    rowspec = pl.BlockSpec((tb, HEAD), lambda b, t: (b * per_b + t, 0))
    prm = pl.BlockSpec((None, 1, HEAD), lambda b, t: (layer, 0, 0))
    return pl.pallas_call(
        functools.partial(_ret_kernel, tb=tb),
        grid=(batch, per_b),
        in_specs=[_zspec(tb, per_b, G_RET_Q), _zspec(tb, per_b, G_RET_K), _zspec(tb, per_b, G_RET_V),
                  _zspec(tb, per_b, G_RET_GATE), rowspec, rowspec, prm, prm],
        out_specs=pl.BlockSpec((tb, GROUP), lambda b, t: (b * per_b + t, 0)),
        out_shape=jax.ShapeDtypeStruct((m, GROUP), BF16),
        scratch_shapes=[pltpu.VMEM((N_HEADS, HEAD, HEAD), F32)],
        compiler_params=_params(("parallel", "arbitrary"), 8 * tb * GROUP * 4, 2 * tb * GROUP * 2,
                                4 * tb * HEAD * 4, 16 * RET_CHUNK * RET_CHUNK * 4),
        name="retention",
    )(z, z, z, z, cos2, sin2, norm_g, norm_b)


def _merge_kernel(u_ref, y0, y1, y2, y3, g0, g1, g2, g3, b0, b1, b2, b3, wb_ref, o_ref):
    u = u_ref[...]
    acc = None
    for m, (y, wg, bg) in enumerate(((y0, g0, b0), (y1, g1, b1), (y2, g2, b2), (y3, g3, b3))):
        gate = _sigmoid(lax.dot_general(u, wg[...], _NN, preferred_element_type=F32) + bg[...])
        term = gate * lax.dot_general(y[...], wb_ref[m], _NN, preferred_element_type=F32)
        acc = term if acc is None else acc + term
    o_ref[...] = acc.astype(BF16)


def _merge(u, ys, w_gate, b_gate, w_branch, layer):
    m, d = u.shape
    tm = _pick(m, (1024, 512, 256, 128))
    td = _pick(d, (512, 256, 128))
    nd = d // td
    yspec = pl.BlockSpec((tm, GROUP), lambda i, j: (i, 0))
    gspecs = [pl.BlockSpec((None, d, td), functools.partial(lambda i, j, br: (layer, 0, br * nd + j), br=br))
              for br in range(N_BRANCH)]
    bspecs = [pl.BlockSpec((None, 1, td), functools.partial(lambda i, j, br: (layer, 0, br * nd + j), br=br))
              for br in range(N_BRANCH)]
    return pl.pallas_call(
        _merge_kernel,
        grid=(m // tm, nd),
        in_specs=[pl.BlockSpec((tm, d), lambda i, j: (i, 0))] + [yspec] * N_BRANCH + gspecs + bspecs
                 + [pl.BlockSpec((None, N_BRANCH, GROUP, td), lambda i, j: (layer, 0, 0, j))],
        out_specs=pl.BlockSpec((tm, td), lambda i, j: (i, j)),
        out_shape=jax.ShapeDtypeStruct((m, d), BF16),
        compiler_params=_params(("parallel", "arbitrary"), 2 * tm * d * 2, 2 * N_BRANCH * tm * GROUP * 2,
                                2 * N_BRANCH * d * td * 2, 2 * N_BRANCH * GROUP * td * 2, 2 * tm * td * 2,
                                6 * tm * td * 4),
        name="branch_merge",
    )(u, *ys, *([w_gate] * N_BRANCH), *([b_gate] * N_BRANCH), w_branch)


def _outproj_kernel(x_ref, w_ref, h_ref, gate_ref, o_ref):
    o_ref[...] = h_ref[...] + gate_ref[...] * lax.dot_general(x_ref[...], w_ref[...], _NN, preferred_element_type=F32)


def _out_projection(x, w_out, h, gate, layer, seq):
    m, d = h.shape
    tm = _pick(seq, (512, 256, 128))
    per_b = seq // tm
    return pl.pallas_call(
        _outproj_kernel,
        grid=(m // tm,),
        in_specs=[pl.BlockSpec((tm, d), lambda i: (i, 0)),
                  pl.BlockSpec((None, d, d), lambda i: (layer, 0, 0)),
                  pl.BlockSpec((tm, d), lambda i: (i, 0)),
                  pl.BlockSpec((None, 1, d), lambda i: (i // per_b, 0, 0))],
        out_specs=pl.BlockSpec((tm, d), lambda i: (i, 0)),
        out_shape=jax.ShapeDtypeStruct((m, d), F32),
        compiler_params=_params(("parallel",), 2 * tm * d * 2, 2 * d * d * 2, 4 * tm * d * 4, 2 * tm * d * 4),
        name="out_projection",
    )(x, w_out, h, gate)


def _ffn_kernel(h_ref, g_ref, sc_ref, sh_ref, gt_ref, wg_ref, wu_ref, wd_ref, fg_ref, o_ref, u_s, *, final):
    f = pl.program_id(1)

    @pl.when(f == 0)
    def _():
        u_s[...] = _norm_mod(h_ref[...], g_ref[...], sc_ref[...], sh_ref[...])
        o_ref[...] = jnp.zeros(o_ref.shape, F32)

    u = u_s[...]
    a = lax.dot_general(u, wg_ref[...], _NN, preferred_element_type=F32)
    b = lax.dot_general(u, wu_ref[...], _NN, preferred_element_type=F32)
    o_ref[...] += lax.dot_general((_silu(a) * b).astype(BF16), wd_ref[...], _NN, preferred_element_type=F32)

    @pl.when(f == pl.num_programs(1) - 1)
    def _():
        hn = h_ref[...] + gt_ref[...] * o_ref[...]
        o_ref[...] = _rms(hn, fg_ref[...]) if final else hn


def _ffn(h, gain, scale, shift, gate, w_gate, w_up, w_down, final_gain, layer, seq, final):
    m, d = h.shape
    dff = w_gate.shape[-1]
    tm = _pick(seq, (512, 256, 128))
    tf = _pick(dff, (512, 256, 128))
    per_b = seq // tm
    mod = pl.BlockSpec((None, 1, d), lambda i, f: (i // per_b, 0, 0))
    return pl.pallas_call(
        functools.partial(_ffn_kernel, final=final),
        grid=(m // tm, dff // tf),
        in_specs=[pl.BlockSpec((tm, d), lambda i, f: (i, 0)),
                  pl.BlockSpec((None, 1, d), lambda i, f: (layer, 0, 0)),
                  mod, mod, mod,
                  pl.BlockSpec((None, d, tf), lambda i, f: (layer, 0, f)),
                  pl.BlockSpec((None, d, tf), lambda i, f: (layer, 0, f)),
                  pl.BlockSpec((None, tf, d), lambda i, f: (layer, f, 0)),
                  pl.BlockSpec((1, d), lambda i, f: (0, 0))],
        out_specs=pl.BlockSpec((tm, d), lambda i, f: (i, 0)),
        out_shape=jax.ShapeDtypeStruct((m, d), F32),
        scratch_shapes=[pltpu.VMEM((tm, d), BF16)],
        compiler_params=_params(("parallel", "arbitrary"), 4 * tm * d * 4, 6 * d * tf * 2, tm * d * 2,
                                4 * tm * tf * 4),
        name="swiglu_ffn",
    )(h, gain, scale, shift, gate, w_gate, w_up, w_down, final_gain)


def _split_offsets():
    widths = (("gdn_q", GROUP), ("gdn_k", GROUP), ("gdn_v", GROUP), ("gdn_a", N_HEADS), ("gdn_b", N_HEADS),
              ("gdn_gate", GROUP), ("hgrn_q", GROUP), ("hgrn_f", GROUP), ("hgrn_i", GROUP), ("hgrn_gate", GROUP),
              ("ret_q", GROUP), ("ret_k", GROUP), ("ret_v", GROUP), ("ret_gate", GROUP),
              ("gla_q", N_HEADS * GLA_DK), ("gla_k", N_HEADS * GLA_DK), ("gla_v", GROUP),
              ("gla_gk_lr", GLA_LOWRANK), ("gla_gate", GROUP))
    out, off = {}, 0
    for name, w in widths:
        out[name] = (off, w)
        off += w
    return out, off


def _permute_w_in(w_in):
    offs, total = _split_offsets()
    assert w_in.shape[-1] == total
    depth, d, _ = w_in.shape
    w_in = w_in.astype(BF16)

    def cols(name):
        o, w = offs[name]
        return w_in[:, :, o:o + w]

    def pad_heads(name):
        x = cols(name).reshape(depth, d, N_HEADS, GLA_DK)
        return jnp.pad(x, ((0, 0), (0, 0), (0, 0), (0, HEAD - GLA_DK))).reshape(depth, d, GROUP)

    big = [cols("gdn_q"), cols("gdn_k"), cols("gdn_v"), cols("gdn_gate"),
           cols("hgrn_q"), cols("hgrn_f"), cols("hgrn_i"), cols("hgrn_gate"),
           cols("ret_q"), cols("ret_k"), cols("ret_v"), cols("ret_gate"),
           pad_heads("gla_q"), pad_heads("gla_k"), cols("gla_v"), cols("gla_gate")]
    small = jnp.concatenate([cols("gdn_a"), cols("gdn_b"), cols("gla_gk_lr")], axis=-1)
    small = jnp.pad(small, ((0, 0), (0, 0), (0, SMALL - small.shape[-1])))
    return jnp.concatenate(big, axis=-1), small


def _pad_gk(gk_w2, gk_b):
    depth = gk_w2.shape[0]
    pad = ((0, 0), (0, 0), (0, 0), (0, HEAD - GLA_DK))
    w2 = jnp.pad(gk_w2.reshape(depth, GLA_LOWRANK, N_HEADS, GLA_DK), pad).reshape(depth, GLA_LOWRANK, GROUP)
    w2 = jnp.pad(w2, ((0, 0), (2 * N_HEADS, SMALL - 2 * N_HEADS - GLA_LOWRANK), (0, 0)))
    b = jnp.pad(gk_b.reshape(depth, 1, N_HEADS, GLA_DK), pad).reshape(depth, 1, GROUP)
    return w2.astype(BF16), b


def _lane_row(x, width=SMALL):
    return jnp.pad(x.astype(F32), ((0, 0), (0, width - x.shape[-1])))[:, None, :]


def kernel(x, c, positions, ada_w, ada_b, norm_mix_g, w_in, gdn_conv_w, gdn_a_log, gdn_dt_bias, gdn_norm_g, hgrn_lower_bounds, hgrn_norm_g, ret_norm_g, ret_norm_b, gla_gk_w2, gla_gk_b, gla_norm_g, w_branch, w_merge_gate, b_merge_gate, w_out, norm_ffn_g, ffn_w_gate, ffn_w_up, ffn_w_down, final_norm_g):
    batch, seq, d = x.shape
    depth = w_in.shape[0]
    m = batch * seq
    assert seq % max(RET_CHUNK, GDN_CHUNK) == 0 and d % LANE == 0

    w_big, w_small = _permute_w_in(w_in)
    w_mg = w_merge_gate.astype(BF16)
    b_mg = b_merge_gate.reshape(depth, 1, N_BRANCH * d)
    w_br = w_branch.astype(BF16)
    w_o = w_out.astype(BF16)
    w_fg, w_fu, w_fd = ffn_w_gate.astype(BF16), ffn_w_up.astype(BF16), ffn_w_down.astype(BF16)
    alog_row, dt_row = _lane_row(gdn_a_log), _lane_row(gdn_dt_bias)
    w2_pad, gkb_pad = _pad_gk(gla_gk_w2, gla_gk_b)
    row3 = lambda a: a.reshape(depth, 1, a.shape[-1])

    mod = _modulation(c, ada_w, ada_b)
    cos2, sin2 = _rope_tables(positions)

    final_gain = final_norm_g.reshape(1, d)
    h = x.reshape(m, d)
    for l in range(depth):
        shift_m, scale_m, gate_m, shift_f, scale_f, gate_f = [mod[l, :, i * d:(i + 1) * d][:, None, :] for i in range(6)]
        u, z, zs = _in_projection(h, row3(norm_mix_g), scale_m, shift_m, w_big, w_small, l, seq)
        ys = (_gdn_mixer(z, zs, gdn_conv_w, alog_row, dt_row, row3(gdn_norm_g), l, batch, seq),
              _hgrn_mixer(z, hgrn_lower_bounds, row3(hgrn_norm_g), l, batch, seq),
              _ret_mixer(z, cos2, sin2, row3(ret_norm_g), row3(ret_norm_b), l, batch, seq),
              _gla_mixer(z, zs, w2_pad, gkb_pad, row3(gla_norm_g), l, batch, seq))
        merged = _merge(u, ys, w_mg, b_mg, w_br, l)
        h = _out_projection(merged, w_o, h, gate_m, l, seq)
        h = _ffn(h, row3(norm_ffn_g), scale_f, shift_f, gate_f, w_fg, w_fu, w_fd, final_gain, l, seq,
                 final=(l == depth - 1))
    return h.reshape(batch, seq, d)
```

```python
import functools
import math

import numpy as np
import jax
import jax.numpy as jnp
from jax import lax
from jax.experimental import pallas as pl
from jax.experimental.pallas import tpu as pltpu

F32 = jnp.float32
BF16 = jnp.bfloat16
HI = lax.Precision.HIGHEST

N_HEADS = 4
HEAD = 128
GROUP = N_HEADS * HEAD
GLA_DK = 64
GLA_LOWRANK = 16
GLA_GATE_NORM = 16.0
CONV_W = 4
EPS = 1e-6
ROPE_BASE = 10000.0
N_BRANCH = 4

GDN_CHUNK = 64
GLA_CHUNK = 128
RET_CHUNK = 256
SMALL = 128

LANE = 128
SUBLANE = 8
VMEM_CAP_BYTES = 60000 * 1024
INTERNAL_SCRATCH_BYTES = 12 * 1024 * 1024

G_GDN_Q, G_GDN_K, G_GDN_V, G_GDN_GATE = 0, 1, 2, 3
G_HGRN_Q, G_HGRN_F, G_HGRN_I, G_HGRN_GATE = 4, 5, 6, 7
G_RET_Q, G_RET_K, G_RET_V, G_RET_GATE = 8, 9, 10, 11
G_GLA_Q, G_GLA_K, G_GLA_V, G_GLA_GATE = 12, 13, 14, 15
N_GROUPS = 16

LOG2_E = 1.4426950408889634
LOG_GAMMA = tuple(float(v) for v in np.log1p(-np.exp(np.linspace(math.log(1.0 / 32.0), math.log(1.0 / 512.0), N_HEADS))))

_NN = (((1,), (0,)), ((), ()))
_NT = (((1,), (1,)), ((), ()))
_TN = (((0,), (0,)), ((), ()))


def _mm(a, b, dims=_NN):
    return lax.dot_general(a.astype(BF16), b.astype(BF16), dims, preferred_element_type=F32)


def _mmh(a, b, dims=_NN):
    return lax.dot_general(a, b, dims, precision=HI, preferred_element_type=F32)


def _sigmoid(x):
    return 1.0 / (1.0 + jnp.exp(-x))


def _silu(x):
    return x * _sigmoid(x)


def _softplus(x):
    return jnp.maximum(x, 0.0) + jnp.log1p(jnp.exp(-jnp.abs(x)))


def _log_sigmoid(x):
    return -_softplus(-x)


def _rms(x, g):
    return x * lax.rsqrt(jnp.mean(x * x, axis=-1, keepdims=True) + EPS) * g


def _seg_cumsum(x, seg):
    row = lax.broadcasted_iota(jnp.int32, x.shape, 0) & (seg - 1)
    s = 1
    while s < seg:
        x = x + jnp.where(row >= s, pltpu.roll(x, s, axis=0), 0.0)
        s *= 2
    return x


def _pick(n, prefs):
    for p in prefs:
        if n % p == 0:
            return p
    return n


def _params(sem, *nbytes):
    need = int(sum(nbytes)) + INTERNAL_SCRATCH_BYTES
    return pltpu.CompilerParams(dimension_semantics=sem, vmem_limit_bytes=min(VMEM_CAP_BYTES, need))


def _mod_kernel(c_ref, w_ref, b_ref, o_ref):
    o_ref[...] = _mm(_silu(c_ref[...]), w_ref[...]) + b_ref[...]


def _modulation(c, ada_w, ada_b):
    depth, d, n = ada_w.shape
    b = c.shape[0]
    rows = -(-b // SUBLANE) * SUBLANE
    c_pad = jnp.zeros((rows, d), F32).at[:b].set(c)
    tn = _pick(n, (1536, 1024, 512, 256, 128))
    out = pl.pallas_call(
        _mod_kernel,
        grid=(depth, n // tn),
        in_specs=[pl.BlockSpec((rows, d), lambda l, j: (0, 0)),
                  pl.BlockSpec((None, d, tn), lambda l, j: (l, 0, j)),
                  pl.BlockSpec((None, 1, tn), lambda l, j: (l, 0, j))],
        out_specs=pl.BlockSpec((None, rows, tn), lambda l, j: (l, 0, j)),
        out_shape=jax.ShapeDtypeStruct((depth, rows, n), F32),
        compiler_params=_params(("parallel", "parallel"), 2 * d * tn * 4, 4 * rows * tn * 4, 8 * rows * d * 4),
        name="adaln_modulation",
    )(c_pad, ada_w, ada_b.reshape(depth, 1, n))
    return out[:, :b]


def _norm_mod(x, g, scale, shift):
    return (_rms(x, g) * (1.0 + scale) + shift).astype(BF16)


def _inproj_kernel(h_ref, g_ref, sc_ref, sh_ref, w_ref, ws_ref, u_ref, z_ref, zs_ref):
    @pl.when(pl.program_id(1) == 0)
    def _():
        u = _norm_mod(h_ref[...], g_ref[...], sc_ref[...], sh_ref[...])
        u_ref[...] = u
        zs_ref[...] = lax.dot_general(u, ws_ref[...], _NN, preferred_element_type=F32)

    z_ref[...] = lax.dot_general(u_ref[...], w_ref[...], _NN, preferred_element_type=F32).astype(z_ref.dtype)


def _in_projection(h, gain, scale, shift, w_big, w_small, layer, seq):
    m, d = h.shape
    nb = w_big.shape[-1]
    tm = _pick(seq, (1024, 512, 256, 128))
    tn = _pick(nb, (1024, 512, 256, 128))
    per_b = seq // tm
    return pl.pallas_call(
        _inproj_kernel,
        grid=(m // tm, nb // tn),
        in_specs=[pl.BlockSpec((tm, d), lambda i, j: (i, 0)),
                  pl.BlockSpec((None, 1, d), lambda i, j: (layer, 0, 0)),
                  pl.BlockSpec((None, 1, d), lambda i, j: (i // per_b, 0, 0)),
                  pl.BlockSpec((None, 1, d), lambda i, j: (i // per_b, 0, 0)),
                  pl.BlockSpec((None, d, tn), lambda i, j: (layer, 0, j)),
                  pl.BlockSpec((None, d, SMALL), lambda i, j: (layer, 0, 0))],
        out_specs=[pl.BlockSpec((tm, d), lambda i, j: (i, 0)),
                   pl.BlockSpec((tm, tn), lambda i, j: (i, j)),
                   pl.BlockSpec((tm, SMALL), lambda i, j: (i, 0))],
        out_shape=[jax.ShapeDtypeStruct((m, d), BF16),
                   jax.ShapeDtypeStruct((m, nb), BF16),
                   jax.ShapeDtypeStruct((m, SMALL), F32)],
        compiler_params=_params(("parallel", "arbitrary"), 2 * tm * d * 4, 2 * tm * d * 2, 2 * d * tn * 2,
                                2 * tm * tn * 2, 2 * d * SMALL * 2, 2 * tm * SMALL * 4, 2 * tm * d * 4),
        name="norm_in_projection",
    )(h, gain, scale, shift, w_big, w_small)


def _head_rms_gate(o, g, gate):
    return (o * lax.rsqrt(jnp.mean(o * o, axis=-1, keepdims=True) + EPS) * g * _silu(gate)).astype(BF16)


def _head_group_gate(o, g, b, gate):
    mu = jnp.mean(o, axis=-1, keepdims=True)
    oc = o - mu
    var = jnp.mean(oc * oc, axis=-1, keepdims=True)
    return ((oc * lax.rsqrt(var + EPS) * g + b) * _silu(gate)).astype(BF16)


def _f32(ref, rows, cols):
    return ref[rows, cols].astype(F32)


def _zspec(tb, per_b, group):
    return pl.BlockSpec((tb, GROUP), lambda b, t: (b * per_b + t, group))


def _causal_conv(x, prev, w):
    acc = x * w[CONV_W - 1:CONV_W]
    rows = lax.broadcasted_iota(jnp.int32, prev.shape, 0)
    for s in range(1, CONV_W):
        xs = pltpu.roll(x, s, axis=0)
        head = jnp.where(rows < s, pltpu.roll(prev, s, axis=0), xs[0:SUBLANE])
        xs = jnp.concatenate([head, xs[SUBLANE:]], axis=0)
        acc = acc + xs * w[CONV_W - 1 - s:CONV_W - s]
    return acc


def _l2norm(x):
    return x * lax.rsqrt(jnp.sum(x * x, axis=-1, keepdims=True) + EPS)


def _split_bf16(a):
    hi = a.astype(BF16)
    return hi, (a - hi.astype(F32)).astype(BF16)


_BNN = (((2,), (1,)), ((0,), (0,)))
_BNT = (((2,), (2,)), ((0,), (0,)))
_BTN = (((1,), (1,)), ((0,), (0,)))


def _bdot(a, b, dims):
    return lax.dot_general(a, b, dims, preferred_element_type=F32)


def _bmm3(a, b, dims=_BNN):
    (ah, al), (bh, bl) = a, b
    return _bdot(ah, bh, dims) + _bdot(ah, bl, dims) + _bdot(al, bh, dims)


def _unit_lower_inverse(lm, ii, jj):
    c = lm.shape[-1]
    x = jnp.where(ii == jj, 1.0, 0.0) - jnp.where(((ii & 1) == 1) & (jj == ii - 1), lm, 0.0)
    lms = _split_bf16(lm)
    k = 1
    while (1 << k) < c:
        same = (ii >> (k + 1)) == (jj >> (k + 1))
        sel = same & (((ii >> k) & 1) == 1) & (((jj >> k) & 1) == 0)
        xs = _split_bf16(x)
        x = x - jnp.where(sel, _bmm3(_split_bf16(_bmm3(xs, lms)), xs), 0.0)
        k += 1
    return x


def _gdn_kernel(zq_ref, zk_ref, zv_ref, zg_ref, zs_ref, cw_ref, alog_ref, dt_ref, ng_ref, o_ref,
                prev_ref, q_s, k_s, v_s, gb_s, u_s, p_s, n_s, st_ref, *, tb):
    c = GDN_CHUNK
    nc = tb // c

    @pl.when(pl.program_id(1) == 0)
    def _():
        prev_ref[...] = jnp.zeros(prev_ref.shape, F32)
        st_ref[...] = jnp.zeros(st_ref.shape, F32)

    cw = cw_ref[...]
    for idx, (src, dst) in enumerate(((zq_ref, q_s), (zk_ref, k_s), (zv_ref, v_s))):
        x = src[...].astype(F32)
        y = _silu(_causal_conv(x, prev_ref[idx], cw[:, idx * GROUP:(idx + 1) * GROUP]))
        prev_ref[idx] = x[tb - SUBLANE:tb]
        for h in range(N_HEADS):
            yh = y[:, h * HEAD:(h + 1) * HEAD]
            if idx == 0:
                yh = _l2norm(yh) * HEAD ** -0.5
            elif idx == 1:
                yh = _l2norm(yh)
            for ci in range(nc):
                dst[h, ci] = yh[ci * c:(ci + 1) * c]

    zs = zs_ref[...]
    lane = lax.broadcasted_iota(jnp.int32, (1, SMALL), 1)
    g = -jnp.exp(alog_ref[...]) * _softplus(zs + dt_ref[...])
    gcum = _seg_cumsum(jnp.where(lane < N_HEADS, g, 0.0), c)
    gball = jnp.where(lane < N_HEADS, gcum, _sigmoid(zs))
    for ci in range(nc):
        gb_s[ci] = gball[ci * c:(ci + 1) * c]

    ii = lax.broadcasted_iota(jnp.int32, (1, c, c), 1)
    jj = lax.broadcasted_iota(jnp.int32, (1, c, c), 2)
    onehot = jnp.where(lax.broadcasted_iota(jnp.int32, (nc, SUBLANE, SMALL), 1)
                       == lax.broadcasted_iota(jnp.int32, (nc, SUBLANE, SMALL), 2), 1.0, 0.0)
    gb = gb_s[...]
    grow = lax.dot_general(onehot, gb, _BNT, precision=HI, preferred_element_type=F32)
    lms = []
    for h in range(N_HEADS):
        k = k_s[h]
        dec = jnp.exp(jnp.where(ii >= jj, gb[:, :, h:h + 1] - grow[:, h:h + 1, :], -jnp.inf))
        kb = (k * gb[:, :, N_HEADS + h:N_HEADS + h + 1]).astype(BF16)
        lms.append(jnp.where(ii > jj, _bdot(kb, k.astype(BF16), _BNT) * dec, 0.0))
    tinv_all = _unit_lower_inverse(jnp.concatenate(lms, axis=0), ii, jj)
    for h in range(N_HEADS):
        q, k, v = q_s[h], k_s[h], v_s[h]
        gcol = gb[:, :, h:h + 1]
        beta = gb[:, :, N_HEADS + h:N_HEADS + h + 1]
        dec = jnp.exp(jnp.where(ii >= jj, gcol - grow[:, h:h + 1, :], -jnp.inf))
        kb = k * beta
        kbf = k.astype(BF16)
        tinv = tinv_all[h * nc:(h + 1) * nc]
        egc = jnp.exp(gcol)
        wu = _bmm3(_split_bf16(tinv), _split_bf16(jnp.concatenate([kb * egc, v * beta], axis=2))).astype(BF16)
        a = (_bdot(q.astype(BF16), kbf, _BNT) * dec).astype(BF16)
        kd = (k * jnp.exp(gcol[:, c - 1:c, :] - gcol)).astype(BF16)
        aw = _bdot(a, wu, _BNN)
        kw = _bdot(kd, wu, _BTN)
        q_s[h] = q * egc - aw[:, :, :HEAD]
        u_s[h] = aw[:, :, HEAD:]
        p_s[h] = kw[:, :, :HEAD]
        n_s[h] = kw[:, :, HEAD:]

    ng = ng_ref[...]
    for ci in range(nc):
        r = slice(ci * c, (ci + 1) * c)
        for h in range(N_HEADS):
            sl = slice(h * HEAD, (h + 1) * HEAD)
            st = st_ref[h]
            o = u_s[h, ci] + _mm(q_s[h, ci], st)
            st_ref[h] = jnp.exp(gb_s[ci, c - 1:c, h:h + 1]) * st - _mm(p_s[h, ci], st) + n_s[h, ci]
            o_ref[r, sl] = _head_rms_gate(o, ng, _f32(zg_ref, r, sl))


def _gdn_mixer(z, zs, conv_w, alog_row, dt_row, norm_g, layer, batch, seq):
    m = z.shape[0]
    tb = _pick(seq, (512, 256, 128, 64))
    per_b = seq // tb
    nc = tb // GDN_CHUNK
    nconv = conv_w.shape[-1]
    row = lambda a: pl.BlockSpec((None, 1, a.shape[-1]), lambda b, t: (layer, 0, 0))
    return pl.pallas_call(
        functools.partial(_gdn_kernel, tb=tb),
        grid=(batch, per_b),
        in_specs=[_zspec(tb, per_b, G_GDN_Q), _zspec(tb, per_b, G_GDN_K), _zspec(tb, per_b, G_GDN_V),
                  _zspec(tb, per_b, G_GDN_GATE),
                  pl.BlockSpec((tb, SMALL), lambda b, t: (b * per_b + t, 0)),
                  pl.BlockSpec((None, CONV_W, nconv), lambda b, t: (layer, 0, 0)),
                  row(alog_row), row(dt_row), row(norm_g)],
        out_specs=pl.BlockSpec((tb, GROUP), lambda b, t: (b * per_b + t, 0)),
        out_shape=jax.ShapeDtypeStruct((m, GROUP), BF16),
        scratch_shapes=[pltpu.VMEM((3, SUBLANE, GROUP), F32)]
                       + [pltpu.VMEM((N_HEADS, nc, GDN_CHUNK, HEAD), F32) for _ in range(3)]
                       + [pltpu.VMEM((nc, GDN_CHUNK, SMALL), F32)]
                       + [pltpu.VMEM((N_HEADS, nc, GDN_CHUNK, HEAD), F32)]
                       + [pltpu.VMEM((N_HEADS, nc, HEAD, HEAD), F32) for _ in range(2)]
                       + [pltpu.VMEM((N_HEADS, HEAD, HEAD), F32)],
        compiler_params=_params(("parallel", "arbitrary"), 8 * tb * GROUP * 4, 2 * tb * GROUP * 2,
                                5 * tb * GROUP * 4, 2 * N_HEADS * nc * HEAD * HEAD * 4, 8 * tb * GROUP * 4),
        name="gated_deltanet",
    )(z, z, z, z, zs, conv_w, alog_row, dt_row, norm_g)


def _gla_levels(tb):
    c = GLA_CHUNK
    row = lax.broadcasted_iota(jnp.int32, (tb, HEAD), 0)
    ii = lax.broadcasted_iota(jnp.int32, (1, c, c), 1)
    jj = lax.broadcasted_iota(jnp.int32, (1, c, c), 2)
    levels = []
    hs, bit = 1, 0
    while hs < c:
        pair = ((ii >> (bit + 1)) == (jj >> (bit + 1))) & ((ii & hs) != 0) & ((jj & hs) == 0)
        levels.append((hs, (row & hs) != 0, pair))
        hs, bit = hs * 2, bit + 1
    return ii == jj, levels


def _gla_head(q, k, v, logf, st_ref, h, zg_ref, ng, o_ref, diag, levels, pending):
    tb = q.shape[0]
    c = GLA_CHUNK
    nc = tb // c
    chunks = lambda a: a.reshape(nc, c, HEAD).astype(BF16)

    b = _seg_cumsum(logf * LOG2_E, c)
    e = b
    scores = jnp.where(diag, _bdot(chunks(q), chunks(k), _BNT), 0.0)
    for hs, upper, pair in levels:
        expo = jnp.minimum(jnp.where(upper, b - pltpu.roll(e, hs, axis=0), e - b), 0.0)
        x = chunks(jnp.where(upper, q, k) * jnp.exp2(expo))
        scores = jnp.where(pair, _bdot(x, x, _BNT), scores)
        e = jnp.where(upper, e, pltpu.roll(e, tb - hs, axis=0))
        if pending:
            pending.pop(0)()
    qd = chunks(q * jnp.exp2(b))
    kd = chunks(k * jnp.exp2(e - b))
    vb = chunks(v)
    intra = _bdot(scores.astype(BF16), vb, _BNN)
    sl = slice(h * HEAD, (h + 1) * HEAD)

    def step(ci):
        r = slice(ci * c, (ci + 1) * c)
        st = st_ref[h]
        o = intra[ci] + lax.dot_general(qd[ci], st.astype(BF16), _NT, preferred_element_type=F32)
        st_ref[h] = st * jnp.exp2(e[ci * c:ci * c + 1, :]) + lax.dot_general(vb[ci], kd[ci], _TN, preferred_element_type=F32)
        o_ref[r, sl] = _head_rms_gate(o, ng, _f32(zg_ref, r, sl))

    while pending:
        pending.pop(0)()
    return [functools.partial(step, ci) for ci in range(nc)]


def _hgrn_kernel(zq_ref, zf_ref, zi_ref, zg_ref, lb_ref, ng_ref, o_ref, st_ref, *, layer):
    @pl.when(pl.program_id(1) == 0)
    def _():
        st_ref[...] = jnp.zeros(st_ref.shape, F32)

    x = lb_ref[...]
    e = jnp.exp(x - jnp.max(x, axis=0, keepdims=True))
    sm = e / jnp.sum(e, axis=0, keepdims=True)
    lb_all = jnp.zeros((1, GROUP), F32)
    for j in range(1, layer + 1):
        lb_all = lb_all + sm[j:j + 1, :]
    ng = ng_ref[...]
    diag, levels = _gla_levels(zq_ref.shape[0])
    rows = slice(None)
    pending = []
    for h in range(N_HEADS):
        sl = slice(h * HEAD, (h + 1) * HEAD)
        lb = lb_all[:, sl]
        f = lb + (1.0 - lb) * _sigmoid(_f32(zf_ref, rows, sl))
        q = _silu(_f32(zq_ref, rows, sl)) * HEAD ** -0.5
        pending = _gla_head(q, 1.0 - f, _f32(zi_ref, rows, sl), jnp.log(f), st_ref, h, zg_ref, ng, o_ref,
                            diag, levels, pending)
    for step in pending:
        step()


def _gla_kernel(zq_ref, zk_ref, zv_ref, zg_ref, zs_ref, w2_ref, gkb_ref, ng_ref, o_ref, st_ref):
    @pl.when(pl.program_id(1) == 0)
    def _():
        st_ref[...] = jnp.zeros(st_ref.shape, F32)

    lane = lax.broadcasted_iota(jnp.int32, (1, HEAD), 1)
    ng = ng_ref[...]
    diag, levels = _gla_levels(zq_ref.shape[0])
    rows = slice(None)
    gk_all = _log_sigmoid(_mm(zs_ref[...], w2_ref[...]) + gkb_ref[...]) / GLA_GATE_NORM
    pending = []
    for h in range(N_HEADS):
        sl = slice(h * HEAD, (h + 1) * HEAD)
        logf = jnp.where(lane < GLA_DK, gk_all[:, sl], 0.0)
        pending = _gla_head(_f32(zq_ref, rows, sl) * GLA_DK ** -0.5, _f32(zk_ref, rows, sl), _f32(zv_ref, rows, sl),
                            logf, st_ref, h, zg_ref, ng, o_ref, diag, levels, pending)
    for step in pending:
        step()


def _gla_scratch(tb):
    return [pltpu.VMEM((N_HEADS, HEAD, HEAD), F32)]


def _hgrn_mixer(z, lower_bounds, norm_g, layer, batch, seq):
    m = z.shape[0]
    tb = _pick(seq, (512, 256, 128, 64))
    per_b = seq // tb
    depth = lower_bounds.shape[0]
    return pl.pallas_call(
        functools.partial(_hgrn_kernel, layer=layer),
        grid=(batch, per_b),
        in_specs=[_zspec(tb, per_b, G_HGRN_Q), _zspec(tb, per_b, G_HGRN_F), _zspec(tb, per_b, G_HGRN_I),
                  _zspec(tb, per_b, G_HGRN_GATE),
                  pl.BlockSpec((depth, GROUP), lambda b, t: (0, 0)),
                  pl.BlockSpec((None, 1, HEAD), lambda b, t: (layer, 0, 0))],
        out_specs=pl.BlockSpec((tb, GROUP), lambda b, t: (b * per_b + t, 0)),
        out_shape=jax.ShapeDtypeStruct((m, GROUP), BF16),
        scratch_shapes=_gla_scratch(tb),
        compiler_params=_params(("parallel", "arbitrary"), 8 * tb * GROUP * 4, 2 * tb * GROUP * 2,
                                4 * tb * GROUP * 4, 8 * tb * GROUP * 4),
        name="hgrn2",
    )(z, z, z, z, lower_bounds, norm_g)


def _gla_mixer(z, zs, w2_pad, gkb_pad, norm_g, layer, batch, seq):
    m = z.shape[0]
    tb = _pick(seq, (512, 256, 128, 64))
    per_b = seq // tb
    return pl.pallas_call(
        _gla_kernel,
        grid=(batch, per_b),
        in_specs=[_zspec(tb, per_b, G_GLA_Q), _zspec(tb, per_b, G_GLA_K), _zspec(tb, per_b, G_GLA_V),
                  _zspec(tb, per_b, G_GLA_GATE),
                  pl.BlockSpec((tb, SMALL), lambda b, t: (b * per_b + t, 0)),
                  pl.BlockSpec((None, SMALL, GROUP), lambda b, t: (layer, 0, 0)),
                  pl.BlockSpec((None, 1, GROUP), lambda b, t: (layer, 0, 0)),
                  pl.BlockSpec((None, 1, HEAD), lambda b, t: (layer, 0, 0))],
        out_specs=pl.BlockSpec((tb, GROUP), lambda b, t: (b * per_b + t, 0)),
        out_shape=jax.ShapeDtypeStruct((m, GROUP), BF16),
        scratch_shapes=_gla_scratch(tb),
        compiler_params=_params(("parallel", "arbitrary"), 8 * tb * GROUP * 4, 2 * tb * GROUP * 2,
                                4 * tb * GROUP * 4, 8 * tb * GROUP * 4),
        name="gla",
    )(z, z, z, z, zs, w2_pad, gkb_pad, norm_g)


def _rope_kernel(pos_ref, inv_ref, cos_ref, sin_ref):
    ang = pos_ref[...].astype(F32) * inv_ref[...]
    lane = lax.broadcasted_iota(jnp.int32, ang.shape, 1)
    cos_ref[...] = jnp.cos(ang)
    sin_ref[...] = jnp.where(lane < HEAD // 2, -jnp.sin(ang), jnp.sin(ang))


def _rope_tables(positions):
    m = positions.size
    half = HEAD // 2
    inv = (ROPE_BASE ** (-np.arange(half, dtype=np.float64) / half)).astype(np.float32)
    inv2 = jnp.asarray(np.concatenate([inv, inv])[None, :])
    tb = _pick(m, (1024, 512, 256, 128, 64))
    return pl.pallas_call(
        _rope_kernel,
        grid=(m // tb,),
        in_specs=[pl.BlockSpec((tb, 1), lambda i: (i, 0)), pl.BlockSpec((1, HEAD), lambda i: (0, 0))],
        out_specs=[pl.BlockSpec((tb, HEAD), lambda i: (i, 0)), pl.BlockSpec((tb, HEAD), lambda i: (i, 0))],
        out_shape=[jax.ShapeDtypeStruct((m, HEAD), F32), jax.ShapeDtypeStruct((m, HEAD), F32)],
        compiler_params=_params(("parallel",), 2 * tb * LANE * 4, 4 * tb * HEAD * 4, 4 * tb * HEAD * 4),
        name="rope_tables",
    )(positions.reshape(m, 1), inv2)


def _ret_kernel(zq_ref, zk_ref, zv_ref, zg_ref, cos_ref, sin_ref, ng_ref, nb_ref, o_ref, st_ref, *, tb):
    c = RET_CHUNK

    @pl.when(pl.program_id(1) == 0)
    def _():
        st_ref[...] = jnp.zeros(st_ref.shape, F32)

    ii = lax.broadcasted_iota(jnp.int32, (c, c), 0)
    jj = lax.broadcasted_iota(jnp.int32, (c, c), 1)
    pos = lax.broadcasted_iota(jnp.int32, (c, 1), 0).astype(F32)
    ng, nb = ng_ref[...], nb_ref[...]
    for h in range(N_HEADS):
        sl = slice(h * HEAD, (h + 1) * HEAD)
        lg = LOG_GAMMA[h]
        intra = jnp.where(ii >= jj, jnp.exp((ii - jj).astype(F32) * lg), 0.0)
        xi = jnp.exp((pos + 1.0) * lg)
        zeta = jnp.exp((c - 1.0 - pos) * lg)
        for ci in range(tb // c):
            r = slice(ci * c, (ci + 1) * c)
            cos, sin = cos_ref[r, :], sin_ref[r, :]
            xq, xk = _f32(zq_ref, r, sl), _f32(zk_ref, r, sl)
            q = xq * cos + pltpu.roll(xq, HEAD // 2, axis=1) * sin
            k = (xk * cos + pltpu.roll(xk, HEAD // 2, axis=1) * sin) * HEAD ** -0.5
            v = zv_ref[r, sl]
            st = st_ref[h]
            o = _mm(_mm(q, k, _NT) * intra, v) + _mm(q * xi, st)
            st_ref[h] = math.exp(c * lg) * st + _mm(k * zeta, v, _TN)
            o_ref[r, sl] = _head_group_gate(o, ng, nb, _f32(zg_ref, r, sl))


def _ret_mixer(z, cos2, sin2, norm_g, norm_b, layer, batch, seq):
    m = z.shape[0]
    tb = _pick(seq, (512, 256))
    per_b = seq // tb
    rowspec = pl.BlockSpec((tb, HEAD), lambda b, t: (b * per_b + t, 0))
    prm = pl.BlockSpec((None, 1, HEAD), lambda b, t: (layer, 0, 0))
    return pl.pallas_call(
        functools.partial(_ret_kernel, tb=tb),
        grid=(batch, per_b),
        in_specs=[_zspec(tb, per_b, G_RET_Q), _zspec(tb, per_b, G_RET_K), _zspec(tb, per_b, G_RET_V),
                  _zspec(tb, per_b, G_RET_GATE), rowspec, rowspec, prm, prm],
        out_specs=pl.BlockSpec((tb, GROUP), lambda b, t: (b * per_b + t, 0)),
        out_shape=jax.ShapeDtypeStruct((m, GROUP), BF16),
        scratch_shapes=[pltpu.VMEM((N_HEADS, HEAD, HEAD), F32)],
        compiler_params=_params(("parallel", "arbitrary"), 8 * tb * GROUP * 4, 2 * tb * GROUP * 2,
                                4 * tb * HEAD * 4, 16 * RET_CHUNK * RET_CHUNK * 4),
        name="retention",
    )(z, z, z, z, cos2, sin2, norm_g, norm_b)


def _merge_kernel(u_ref, y0, y1, y2, y3, g0, g1, g2, g3, b0, b1, b2, b3, wb_ref, o_ref):
    u = u_ref[...]
    acc = None
    for m, (y, wg, bg) in enumerate(((y0, g0, b0), (y1, g1, b1), (y2, g2, b2), (y3, g3, b3))):
        gate = _sigmoid(lax.dot_general(u, wg[...], _NN, preferred_element_type=F32) + bg[...])
        term = gate * lax.dot_general(y[...], wb_ref[m], _NN, preferred_element_type=F32)
        acc = term if acc is None else acc + term
    o_ref[...] = acc.astype(BF16)


def _merge(u, ys, w_gate, b_gate, w_branch, layer):
    m, d = u.shape
    tm = _pick(m, (1024, 512, 256, 128))
    td = _pick(d, (512, 256, 128))
    nd = d // td
    yspec = pl.BlockSpec((tm, GROUP), lambda i, j: (i, 0))
    gspecs = [pl.BlockSpec((None, d, td), functools.partial(lambda i, j, br: (layer, 0, br * nd + j), br=br))
              for br in range(N_BRANCH)]
    bspecs = [pl.BlockSpec((None, 1, td), functools.partial(lambda i, j, br: (layer, 0, br * nd + j), br=br))
              for br in range(N_BRANCH)]
    return pl.pallas_call(
        _merge_kernel,
        grid=(m // tm, nd),
        in_specs=[pl.BlockSpec((tm, d), lambda i, j: (i, 0))] + [yspec] * N_BRANCH + gspecs + bspecs
                 + [pl.BlockSpec((None, N_BRANCH, GROUP, td), lambda i, j: (layer, 0, 0, j))],
        out_specs=pl.BlockSpec((tm, td), lambda i, j: (i, j)),
        out_shape=jax.ShapeDtypeStruct((m, d), BF16),
        compiler_params=_params(("parallel", "arbitrary"), 2 * tm * d * 2, 2 * N_BRANCH * tm * GROUP * 2,
                                2 * N_BRANCH * d * td * 2, 2 * N_BRANCH * GROUP * td * 2, 2 * tm * td * 2,
                                6 * tm * td * 4),
        name="branch_merge",
    )(u, *ys, *([w_gate] * N_BRANCH), *([b_gate] * N_BRANCH), w_branch)


def _outproj_kernel(x_ref, w_ref, h_ref, gate_ref, o_ref):
    o_ref[...] = h_ref[...] + gate_ref[...] * lax.dot_general(x_ref[...], w_ref[...], _NN, preferred_element_type=F32)


def _out_projection(x, w_out, h, gate, layer, seq):
    m, d = h.shape
    tm = _pick(seq, (512, 256, 128))
    per_b = seq // tm
    return pl.pallas_call(
        _outproj_kernel,
        grid=(m // tm,),
        in_specs=[pl.BlockSpec((tm, d), lambda i: (i, 0)),
                  pl.BlockSpec((None, d, d), lambda i: (layer, 0, 0)),
                  pl.BlockSpec((tm, d), lambda i: (i, 0)),
                  pl.BlockSpec((None, 1, d), lambda i: (i // per_b, 0, 0))],
        out_specs=pl.BlockSpec((tm, d), lambda i: (i, 0)),
        out_shape=jax.ShapeDtypeStruct((m, d), F32),
        compiler_params=_params(("parallel",), 2 * tm * d * 2, 2 * d * d * 2, 4 * tm * d * 4, 2 * tm * d * 4),
        name="out_projection",
    )(x, w_out, h, gate)


def _ffn_kernel(h_ref, g_ref, sc_ref, sh_ref, gt_ref, wg_ref, wu_ref, wd_ref, fg_ref, o_ref, u_s, *, final):
    f = pl.program_id(1)

    @pl.when(f == 0)
    def _():
        u_s[...] = _norm_mod(h_ref[...], g_ref[...], sc_ref[...], sh_ref[...])
        o_ref[...] = jnp.zeros(o_ref.shape, F32)

    u = u_s[...]
    a = lax.dot_general(u, wg_ref[...], _NN, preferred_element_type=F32)
    b = lax.dot_general(u, wu_ref[...], _NN, preferred_element_type=F32)
    o_ref[...] += lax.dot_general((_silu(a) * b).astype(BF16), wd_ref[...], _NN, preferred_element_type=F32)

    @pl.when(f == pl.num_programs(1) - 1)
    def _():
        hn = h_ref[...] + gt_ref[...] * o_ref[...]
        o_ref[...] = _rms(hn, fg_ref[...]) if final else hn


def _ffn(h, gain, scale, shift, gate, w_gate, w_up, w_down, final_gain, layer, seq, final):
    m, d = h.shape
    dff = w_gate.shape[-1]
    tm = _pick(seq, (512, 256, 128))
    tf = _pick(dff, (512, 256, 128))
    per_b = seq // tm
    mod = pl.BlockSpec((None, 1, d), lambda i, f: (i // per_b, 0, 0))
    return pl.pallas_call(
        functools.partial(_ffn_kernel, final=final),
        grid=(m // tm, dff // tf),
        in_specs=[pl.BlockSpec((tm, d), lambda i, f: (i, 0)),
                  pl.BlockSpec((None, 1, d), lambda i, f: (layer, 0, 0)),
                  mod, mod, mod,
                  pl.BlockSpec((None, d, tf), lambda i, f: (layer, 0, f)),
                  pl.BlockSpec((None, d, tf), lambda i, f: (layer, 0, f)),
                  pl.BlockSpec((None, tf, d), lambda i, f: (layer, f, 0)),
                  pl.BlockSpec((1, d), lambda i, f: (0, 0))],
        out_specs=pl.BlockSpec((tm, d), lambda i, f: (i, 0)),
        out_shape=jax.ShapeDtypeStruct((m, d), F32),
        scratch_shapes=[pltpu.VMEM((tm, d), BF16)],
        compiler_params=_params(("parallel", "arbitrary"), 4 * tm * d * 4, 6 * d * tf * 2, tm * d * 2,
                                4 * tm * tf * 4),
        name="swiglu_ffn",
    )(h, gain, scale, shift, gate, w_gate, w_up, w_down, final_gain)


def _split_offsets():
    widths = (("gdn_q", GROUP), ("gdn_k", GROUP), ("gdn_v", GROUP), ("gdn_a", N_HEADS), ("gdn_b", N_HEADS),
              ("gdn_gate", GROUP), ("hgrn_q", GROUP), ("hgrn_f", GROUP), ("hgrn_i", GROUP), ("hgrn_gate", GROUP),
              ("ret_q", GROUP), ("ret_k", GROUP), ("ret_v", GROUP), ("ret_gate", GROUP),
              ("gla_q", N_HEADS * GLA_DK), ("gla_k", N_HEADS * GLA_DK), ("gla_v", GROUP),
              ("gla_gk_lr", GLA_LOWRANK), ("gla_gate", GROUP))
    out, off = {}, 0
    for name, w in widths:
        out[name] = (off, w)
        off += w
    return out, off


def _permute_w_in(w_in):
    offs, total = _split_offsets()
    assert w_in.shape[-1] == total
    depth, d, _ = w_in.shape
    w_in = w_in.astype(BF16)

    def cols(name):
        o, w = offs[name]
        return w_in[:, :, o:o + w]

    def pad_heads(name):
        x = cols(name).reshape(depth, d, N_HEADS, GLA_DK)
        return jnp.pad(x, ((0, 0), (0, 0), (0, 0), (0, HEAD - GLA_DK))).reshape(depth, d, GROUP)

    big = [cols("gdn_q"), cols("gdn_k"), cols("gdn_v"), cols("gdn_gate"),
           cols("hgrn_q"), cols("hgrn_f"), cols("hgrn_i"), cols("hgrn_gate"),
           cols("ret_q"), cols("ret_k"), cols("ret_v"), cols("ret_gate"),
           pad_heads("gla_q"), pad_heads("gla_k"), cols("gla_v"), cols("gla_gate")]
    small = jnp.concatenate([cols("gdn_a"), cols("gdn_b"), cols("gla_gk_lr")], axis=-1)
    small = jnp.pad(small, ((0, 0), (0, 0), (0, SMALL - small.shape[-1])))
    return jnp.concatenate(big, axis=-1), small


def _pad_gk(gk_w2, gk_b):
    depth = gk_w2.shape[0]
    pad = ((0, 0), (0, 0), (0, 0), (0, HEAD - GLA_DK))
    w2 = jnp.pad(gk_w2.reshape(depth, GLA_LOWRANK, N_HEADS, GLA_DK), pad).reshape(depth, GLA_LOWRANK, GROUP)
    w2 = jnp.pad(w2, ((0, 0), (2 * N_HEADS, SMALL - 2 * N_HEADS - GLA_LOWRANK), (0, 0)))
    b = jnp.pad(gk_b.reshape(depth, 1, N_HEADS, GLA_DK), pad).reshape(depth, 1, GROUP)
    return w2.astype(BF16), b


def _lane_row(x, width=SMALL):
    return jnp.pad(x.astype(F32), ((0, 0), (0, width - x.shape[-1])))[:, None, :]


def kernel(x, c, positions, ada_w, ada_b, norm_mix_g, w_in, gdn_conv_w, gdn_a_log, gdn_dt_bias, gdn_norm_g, hgrn_lower_bounds, hgrn_norm_g, ret_norm_g, ret_norm_b, gla_gk_w2, gla_gk_b, gla_norm_g, w_branch, w_merge_gate, b_merge_gate, w_out, norm_ffn_g, ffn_w_gate, ffn_w_up, ffn_w_down, final_norm_g):
    batch, seq, d = x.shape
    depth = w_in.shape[0]
    m = batch * seq
    assert seq % max(RET_CHUNK, GDN_CHUNK) == 0 and d % LANE == 0

    w_big, w_small = _permute_w_in(w_in)
    w_mg = w_merge_gate.astype(BF16)
    b_mg = b_merge_gate.reshape(depth, 1, N_BRANCH * d)
    w_br = w_branch.astype(BF16)
    w_o = w_out.astype(BF16)
    w_fg, w_fu, w_fd = ffn_w_gate.astype(BF16), ffn_w_up.astype(BF16), ffn_w_down.astype(BF16)
    alog_row, dt_row = _lane_row(gdn_a_log), _lane_row(gdn_dt_bias)
    w2_pad, gkb_pad = _pad_gk(gla_gk_w2, gla_gk_b)
    row3 = lambda a: a.reshape(depth, 1, a.shape[-1])

    mod = _modulation(c, ada_w, ada_b)
    cos2, sin2 = _rope_tables(positions)

    final_gain = final_norm_g.reshape(1, d)
    h = x.reshape(m, d)
    for l in range(depth):
        shift_m, scale_m, gate_m, shift_f, scale_f, gate_f = [mod[l, :, i * d:(i + 1) * d][:, None, :] for i in range(6)]
        u, z, zs = _in_projection(h, row3(norm_mix_g), scale_m, shift_m, w_big, w_small, l, seq)
        ys = (_gdn_mixer(z, zs, gdn_conv_w, alog_row, dt_row, row3(gdn_norm_g), l, batch, seq),
              _hgrn_mixer(z, hgrn_lower_bounds, row3(hgrn_norm_g), l, batch, seq),
              _ret_mixer(z, cos2, sin2, row3(ret_norm_g), row3(ret_norm_b), l, batch, seq),
              _gla_mixer(z, zs, w2_pad, gkb_pad, row3(gla_norm_g), l, batch, seq))
        merged = _merge(u, ys, w_mg, b_mg, w_br, l)
        h = _out_projection(merged, w_o, h, gate_m, l, seq)
        h = _ffn(h, row3(norm_ffn_g), scale_f, shift_f, gate_f, w_fg, w_fu, w_fd, final_gain, l, seq,
                 final=(l == depth - 1))
    return h.reshape(batch, seq, d)
```
